```python
import math
import jax
import jax.numpy as jnp
from jax import lax
import numpy as np

D_MODEL = 2048
BATCH = 4
SEQ = 8192
DEPTH = 4

GRID_W = 64
CTX_LEN = 256
N_MIXERS = 3
Q_BLOCK = 128
ROPE_BASE = 10000.0
LN_EPS = 1e-5
RMS_EPS = 1e-6
NEG_INF = -1e30

DA_DIM = 128
DA_HEADS = D_MODEL // (2 * DA_DIM)
DA_VDIM = 2 * DA_DIM

WA_DIM = 128
WA_HEADS = D_MODEL // WA_DIM
WA_KV_HEADS = WA_HEADS // 4
WINDOW = 128

MLA_HEADS = D_MODEL // 128
MLA_Q_RANK = D_MODEL // 4
MLA_KV_RANK = D_MODEL // 4
MLA_NOPE = 128
MLA_ROPE = 64
MLA_VDIM = 128

FFN_DIM = 256 * (-(-(8 * D_MODEL // 3) // 256))
N_EXPERTS = 8
TOP_K = 2
EXPERT_DIM = FFN_DIM

kernel_name = 'hybrid_diffusion_trunk'


def _layer_norm(x, g, b):
    xf = x.astype(jnp.float32)
    mu = jnp.mean(xf, axis=-1, keepdims=True)
    var = jnp.mean(jnp.square(xf - mu), axis=-1, keepdims=True)
    return ((xf - mu) * lax.rsqrt(var + LN_EPS) * g + b).astype(x.dtype)


def _rms_norm(x, g):
    xf = x.astype(jnp.float32)
    return (xf * lax.rsqrt(jnp.mean(xf * xf, axis=-1, keepdims=True) + RMS_EPS) * g).astype(x.dtype)


def _modulate(t, shift, scale):
    return t * (1.0 + scale) + shift


def _grid_positions(n):
    rows = n // GRID_W
    row = jnp.repeat(jnp.arange(rows, dtype=jnp.float32), GRID_W)
    col = jnp.tile(jnp.arange(GRID_W, dtype=jnp.float32), rows)
    return row, col


def _rope_1d(v, pos):
    half = v.shape[-1] // 2
    inv = ROPE_BASE ** (-jnp.arange(half, dtype=jnp.float32) / half)
    ang = pos[:, None] * inv[None, :]
    cos = jnp.cos(ang)[:, None, :].astype(v.dtype)
    sin = jnp.sin(ang)[:, None, :].astype(v.dtype)
    v1, v2 = v[..., :half], v[..., half:]
    return jnp.concatenate([v1 * cos - v2 * sin, v2 * cos + v1 * sin], axis=-1)


def _rope_2d(v, row, col):
    half = v.shape[-1] // 2
    return jnp.concatenate([_rope_1d(v[..., :half], row), _rope_1d(v[..., half:], col)], axis=-1)


def _to_blocks(a):
    B, n = a.shape[:2]
    return jnp.moveaxis(a.reshape((B, n // Q_BLOCK, Q_BLOCK) + a.shape[2:]), 1, 0)


def _from_blocks(o):
    nb, B = o.shape[:2]
    return jnp.moveaxis(o, 0, 1).reshape((B, nb * o.shape[2]) + o.shape[3:])


def _lambda_init(layer):
    return 0.8 - 0.6 * math.exp(-0.3 * layer)


def _diff_attend(q, k, v, lam_full, subln_g, lambda_init):
    s = jnp.einsum('bqhcd,bkhcd->bhcqk', q, k).astype(jnp.float32) * (DA_DIM ** -0.5)
    p = jax.nn.softmax(s, axis=-1)
    a = (p[:, :, 0] - lam_full * p[:, :, 1]).astype(v.dtype)
    o = jnp.einsum('bhqk,bkhe->bqhe', a, v)
    return _rms_norm(o, subln_g) * (1.0 - lambda_init)


def _diff_mixer(h, hc, wqkv, lam, subln_g, wo, lambda_init, row, col, need_ctx):
    B, n, _ = h.shape
    m = hc.shape[1]
    q, k, v = jnp.split(h @ wqkv, 3, axis=-1)
    qc, kc, vc = jnp.split(hc @ wqkv, 3, axis=-1)
    q = _rope_2d(q.reshape(B, n, 2 * DA_HEADS, DA_DIM), row, col).reshape(B, n, DA_HEADS, 2, DA_DIM)
    k = _rope_2d(k.reshape(B, n, 2 * DA_HEADS, DA_DIM), row, col).reshape(B, n, DA_HEADS, 2, DA_DIM)
    v = v.reshape(B, n, DA_HEADS, DA_VDIM)
    kc = kc.reshape(B, m, DA_HEADS, 2, DA_DIM)
    vc = vc.reshape(B, m, DA_HEADS, DA_VDIM)
    lamf = lam.astype(jnp.float32)
    lam_full = jnp.exp(jnp.sum(lamf[0] * lamf[1])) - jnp.exp(jnp.sum(lamf[2] * lamf[3])) + lambda_init
    k_all = jnp.concatenate([kc, k], axis=1)
    v_all = jnp.concatenate([vc, v], axis=1)
    o = _from_blocks(lax.map(lambda qb: _diff_attend(qb, k_all, v_all, lam_full, subln_g, lambda_init),
                             _to_blocks(q)))
    y = o.reshape(B, n, DA_HEADS * DA_VDIM) @ wo
    yc = None
    if need_ctx:
        qc = qc.reshape(B, m, DA_HEADS, 2, DA_DIM)
        yc = _diff_attend(qc, kc, vc, lam_full, subln_g, lambda_init).reshape(B, m, DA_HEADS * DA_VDIM) @ wo
    return y, yc


def _sink_attend(q, k, v, sink, valid):
    s = jnp.einsum('bqgrd,bkgd->bgrqk', q, k).astype(jnp.float32) * (q.shape[-1] ** -0.5)
    if valid is not None:
        s = jnp.where(valid, s, NEG_INF)
    sk = sink[None, :, :, None, None]
    mx = jnp.maximum(jnp.max(s, axis=-1, keepdims=True), sk)
    e = jnp.exp(s - mx)
    p = e / (jnp.sum(e, axis=-1, keepdims=True) + jnp.exp(sk - mx))
    return jnp.einsum('bgrqk,bkgd->bqgrd', p.astype(v.dtype), v)


def _window_mixer(h, hc, wqkv, sink, wo, row, col, need_ctx):
    B, n, _ = h.shape
    m = hc.shape[1]
    G, R = WA_KV_HEADS, WA_HEADS // WA_KV_HEADS
    cuts = [WA_HEADS * WA_DIM, (WA_HEADS + WA_KV_HEADS) * WA_DIM]
    q, k, v = jnp.split(h @ wqkv, cuts, axis=-1)
    qc, kc, vc = jnp.split(hc @ wqkv, cuts, axis=-1)
    q = _rope_2d(q.reshape(B, n, WA_HEADS, WA_DIM), row, col).reshape(B, n, G, R, WA_DIM)
    k = _rope_2d(k.reshape(B, n, G, WA_DIM), row, col)
    v = v.reshape(B, n, G, WA_DIM)
    kc = kc.reshape(B, m, G, WA_DIM)
    vc = vc.reshape(B, m, G, WA_DIM)
    sink_gr = sink.astype(jnp.float32).reshape(G, R)
    span = Q_BLOCK + 2 * WINDOW
    pad = ((0, 0), (WINDOW, WINDOW), (0, 0), (0, 0))
    k_pad, v_pad = jnp.pad(k, pad), jnp.pad(v, pad)
    q_idx = jnp.arange(Q_BLOCK)[:, None]
    p_idx = jnp.arange(span)[None, :]
    band = (p_idx >= q_idx) & (p_idx <= q_idx + 2 * WINDOW)
    ctx_ok = jnp.ones((Q_BLOCK, m), dtype=bool)

    def block(args):
        b, qb = args
        start = b * Q_BLOCK
        kw = jnp.concatenate([lax.dynamic_slice_in_dim(k_pad, start, span, axis=1), kc], axis=1)
        vw = jnp.concatenate([lax.dynamic_slice_in_dim(v_pad, start, span, axis=1), vc], axis=1)
        key_pos = start - WINDOW + p_idx
        valid = jnp.concatenate([band & (key_pos >= 0) & (key_pos < n), ctx_ok], axis=1)
        return _sink_attend(qb, kw, vw, sink_gr, valid)

    o = _from_blocks(lax.map(block, (jnp.arange(n // Q_BLOCK), _to_blocks(q))))
    y = o.reshape(B, n, WA_HEADS * WA_DIM) @ wo
    yc = None
    if need_ctx:
        qc = qc.reshape(B, m, G, R, WA_DIM)
        yc = _sink_attend(qc, kc, vc, sink_gr, None).reshape(B, m, WA_HEADS * WA_DIM) @ wo
    return y, yc


def _mla_project(t, w_down, q_norm_g, kv_norm_g, w_uq, w_ukv):
    B, m, _ = t.shape
    cq, ckv, kr = jnp.split(t @ w_down, [MLA_Q_RANK, MLA_Q_RANK + MLA_KV_RANK], axis=-1)
    q = (_rms_norm(cq, q_norm_g) @ w_uq).reshape(B, m, MLA_HEADS, MLA_NOPE + MLA_ROPE)
    kv = (_rms_norm(ckv, kv_norm_g) @ w_ukv).reshape(B, m, MLA_HEADS, MLA_NOPE + MLA_VDIM)
    return q[..., :MLA_NOPE], q[..., MLA_NOPE:], kv[..., :MLA_NOPE], kr[:, :, None, :], kv[..., MLA_NOPE:]


def _mla_attend(qn, qr, kn, kr, v):
    s = (jnp.einsum('bqhd,bkhd->bhqk', qn, kn) + jnp.einsum('bqhd,bkd->bhqk', qr, kr)).astype(jnp.float32)
    p = jax.nn.softmax(s * ((MLA_NOPE + MLA_ROPE) ** -0.5), axis=-1).astype(v.dtype)
    return jnp.einsum('bhqk,bkhd->bqhd', p, v)


def _mla_mixer(h, hc, w_down, q_norm_g, kv_norm_g, w_uq, w_ukv, wo, row, col, need_ctx):
    B, n, _ = h.shape
    m = hc.shape[1]
    qn, qr, kn, kr, v = _mla_project(h, w_down, q_norm_g, kv_norm_g, w_uq, w_ukv)
    qr = _rope_2d(qr, row, col)
    kr = _rope_2d(kr, row, col)[:, :, 0]
    qnc, qrc, knc, krc, vc = _mla_project(hc, w_down, q_norm_g, kv_norm_g, w_uq, w_ukv)
    krc = krc[:, :, 0]
    kn_all = jnp.concatenate([knc, kn], axis=1)
    kr_all = jnp.concatenate([krc, kr], axis=1)
    v_all = jnp.concatenate([vc, v], axis=1)
    o = _from_blocks(lax.map(lambda a: _mla_attend(a[0], a[1], kn_all, kr_all, v_all),
                             (_to_blocks(qn), _to_blocks(qr))))
    y = o.reshape(B, n, MLA_HEADS * MLA_VDIM) @ wo
    yc = None
    if need_ctx:
        yc = _mla_attend(qnc, qrc, knc, krc, vc).reshape(B, m, MLA_HEADS * MLA_VDIM) @ wo
    return y, yc


def _swiglu(h, w13, w2):
    g, u = jnp.split(h @ w13, 2, axis=-1)
    return (jax.nn.silu(g) * u) @ w2


def _moe(h, router, w13, w2):
    logits = (h @ router).astype(jnp.float32)
    top_v, top_i = lax.top_k(logits, TOP_K)
    top_w = jax.nn.softmax(top_v, axis=-1)
    gates = jnp.sum(jax.nn.one_hot(top_i, N_EXPERTS, dtype=jnp.float32) * top_w[..., None], axis=-2).astype(h.dtype)
    out = jnp.zeros_like(h)
    for e in range(N_EXPERTS):
        out = out + gates[..., e:e + 1] * _swiglu(h, w13[e], w2[e])
    return out


def _token_mix(layer, h, hc, need_ctx, row, col, da_wqkv, da_lambda, da_subln_g, da_wo,
               wa_wqkv, wa_sink, wa_wo, mla_w_down, mla_q_norm_g, mla_kv_norm_g, mla_w_uq, mla_w_ukv, mla_wo):
    kind, slot = layer % N_MIXERS, layer // N_MIXERS
    if kind == 0:
        return _diff_mixer(h, hc, da_wqkv[slot], da_lambda[slot], da_subln_g[slot], da_wo[slot],
                           _lambda_init(layer), row, col, need_ctx)
    if kind == 1:
        return _window_mixer(h, hc, wa_wqkv[slot], wa_sink[slot], wa_wo[slot], row, col, need_ctx)
    return _mla_mixer(h, hc, mla_w_down[slot], mla_q_norm_g[slot], mla_kv_norm_g[slot], mla_w_uq[slot],
                      mla_w_ukv[slot], mla_wo[slot], row, col, need_ctx)


def _channel_mix(layer, t, ffn_w13, ffn_w2, moe_router, moe_w13, moe_w2):
    slot = layer // 2
    if layer % 2 == 0:
        return _swiglu(t, ffn_w13[slot], ffn_w2[slot])
    return _moe(t, moe_router[slot], moe_w13[slot], moe_w2[slot])


def setup_inputs(seed: int = 0) -> dict:
    key = jax.random.key(seed)
    keys = iter(jax.random.split(key, 32))
    beta = (8.0 * DEPTH) ** -0.25
    D, F = D_MODEL, FFN_DIM
    n_a = len(range(0, DEPTH, N_MIXERS))
    n_b = len(range(1, DEPTH, N_MIXERS))
    n_c = len(range(2, DEPTH, N_MIXERS))
    n_dense = len(range(0, DEPTH, 2))
    n_moe = len(range(1, DEPTH, 2))

    def normal(shape, scale):
        return jax.random.normal(next(keys), shape, jnp.float32) * scale

    def gain(shape):
        return 1.0 + normal(shape, 0.02)

    return {
        'x': normal((BATCH, SEQ, D), 1.0),
        'c': normal((BATCH, D), 1.0),
        'ctx': normal((BATCH, CTX_LEN, D), 1.0),
        'c_ctx': normal((D,), 1.0),
        'mod_w': normal((DEPTH, D, 6 * D), D ** -0.5),
        'mod_b': normal((DEPTH, 6 * D), 0.02),
        'ln1_g': gain((DEPTH, D)),
        'ln1_b': normal((DEPTH, D), 0.02),
        'ln2_g': gain((DEPTH, D)),
        'ln2_b': normal((DEPTH, D), 0.02),
        'da_wqkv': normal((n_a, D, 3 * 2 * DA_HEADS * DA_DIM), D ** -0.5),
        'da_lambda': normal((n_a, 4, DA_DIM), 0.1),
        'da_subln_g': gain((n_a, DA_VDIM)),
        'da_wo': normal((n_a, DA_HEADS * DA_VDIM, D), beta * (DA_HEADS * DA_VDIM) ** -0.5),
        'wa_wqkv': normal((n_b, D, (WA_HEADS + 2 * WA_KV_HEADS) * WA_DIM), D ** -0.5),
        'wa_sink': normal((n_b, WA_HEADS), 0.5),
        'wa_wo': normal((n_b, WA_HEADS * WA_DIM, D), beta * (WA_HEADS * WA_DIM) ** -0.5),
        'mla_w_down': normal((n_c, D, MLA_Q_RANK + MLA_KV_RANK + MLA_ROPE), D ** -0.5),
        'mla_q_norm_g': gain((n_c, MLA_Q_RANK)),
        'mla_kv_norm_g': gain((n_c, MLA_KV_RANK)),
        'mla_w_uq': normal((n_c, MLA_Q_RANK, MLA_HEADS * (MLA_NOPE + MLA_ROPE)), MLA_Q_RANK ** -0.5),
        'mla_w_ukv': normal((n_c, MLA_KV_RANK, MLA_HEADS * (MLA_NOPE + MLA_VDIM)), MLA_KV_RANK ** -0.5),
        'mla_wo': normal((n_c, MLA_HEADS * MLA_VDIM, D), beta * (MLA_HEADS * MLA_VDIM) ** -0.5),
        'ffn_w13': normal((n_dense, D, 2 * F), D ** -0.5),
        'ffn_w2': normal((n_dense, F, D), beta * F ** -0.5),
        'moe_router': normal((n_moe, D, N_EXPERTS), D ** -0.5),
        'moe_w13': normal((n_moe, N_EXPERTS, D, 2 * EXPERT_DIM), D ** -0.5),
        'moe_w2': normal((n_moe, N_EXPERTS, EXPERT_DIM, D), beta * EXPERT_DIM ** -0.5),
    }


def reference(x, c, ctx, c_ctx, mod_w, mod_b, ln1_g, ln1_b, ln2_g, ln2_b,
              da_wqkv, da_lambda, da_subln_g, da_wo, wa_wqkv, wa_sink, wa_wo,
              mla_w_down, mla_q_norm_g, mla_kv_norm_g, mla_w_uq, mla_w_ukv, mla_wo,
              ffn_w13, ffn_w2, moe_router, moe_w13, moe_w2):
    alpha = (2.0 * DEPTH) ** 0.25
    row, col = _grid_positions(x.shape[1])
    xc = ctx
    for i in range(DEPTH):
        need_ctx = i < DEPTH - 1
        mod = jnp.split((jax.nn.silu(c) @ mod_w[i] + mod_b[i])[:, None, :], 6, axis=-1)
        modc = jnp.split(jax.nn.silu(c_ctx) @ mod_w[i] + mod_b[i], 6, axis=-1)
        y, yc = _token_mix(i, _modulate(x, mod[0], mod[1]), _modulate(xc, modc[0], modc[1]), need_ctx,
                           row, col, da_wqkv, da_lambda, da_subln_g, da_wo, wa_wqkv, wa_sink, wa_wo,
                           mla_w_down, mla_q_norm_g, mla_kv_norm_g, mla_w_uq, mla_w_ukv, mla_wo)
        x = _layer_norm(alpha * x + mod[2] * y, ln1_g[i], ln1_b[i])
        f = _channel_mix(i, _modulate(x, mod[3], mod[4]), ffn_w13, ffn_w2, moe_router, moe_w13, moe_w2)
        x = _layer_norm(alpha * x + mod[5] * f, ln2_g[i], ln2_b[i])
        if need_ctx:
            xc = _layer_norm(alpha * xc + modc[2] * yc, ln1_g[i], ln1_b[i])
            fc = _channel_mix(i, _modulate(xc, modc[3], modc[4]), ffn_w13, ffn_w2, moe_router, moe_w13, moe_w2)
            xc = _layer_norm(alpha * xc + modc[5] * fc, ln2_g[i], ln2_b[i])
    return x
```

```python
import functools
import math

import jax
import jax.numpy as jnp
from jax import lax
from jax.experimental import pallas as pl
from jax.experimental.pallas import tpu as pltpu

F32 = jnp.float32
BF16 = jnp.bfloat16

LANE = 128
SUBLANE = 8
VMEM_LIMIT = 56 * 1024 * 1024

GRID_W = 64
ROPE_BASE = 10000.0
LN_EPS = 1e-5
RMS_EPS = 1e-6
NEG_INF = -1e30
HEAD = 128
WINDOW = 128
MLA_ROPE = 64
N_MIXERS = 3

TM = 512
TN = 1024
TF = 512
TQ = 1024
TK = 768


def _tile(dim, pref, align=SUBLANE):
    if dim <= pref:
        return dim
    t = (pref // align) * align
    while t >= align:
        if dim % t == 0:
            return t
        t -= align
    return dim


def _params(*sem):
    return pltpu.CompilerParams(dimension_semantics=sem, vmem_limit_bytes=VMEM_LIMIT)


def _mod_row(ctx, nb):
    return (lambda b: nb) if ctx else (lambda b: b)


def _mod_kernel(s_ref, w_ref, b_ref, o_ref):
    s = s_ref[...]
    s = s * jax.nn.sigmoid(s)
    o_ref[0] = jnp.dot(s.astype(BF16), w_ref[0].astype(BF16), preferred_element_type=F32) + b_ref[0]


def _modulation(cc, mod_w, mod_b):
    depth, D, N = mod_w.shape
    R = cc.shape[0]
    tn = _tile(N, 1024, LANE)
    return pl.pallas_call(
        _mod_kernel,
        grid=(depth, N // tn),
        in_specs=[pl.BlockSpec((R, D), lambda l, j: (0, 0)),
                  pl.BlockSpec((1, D, tn), lambda l, j: (l, 0, j)),
                  pl.BlockSpec((1, 1, tn), lambda l, j: (l, 0, j))],
        out_specs=pl.BlockSpec((1, R, tn), lambda l, j: (l, 0, j)),
        out_shape=jax.ShapeDtypeStruct((depth, R, N), F32),
        compiler_params=_params("parallel", "parallel"),
        name="modulation",
    )(cc, mod_w, mod_b.reshape(depth, 1, N))


def _rope_tables(n, dim):
    half = dim // 4
    t = jnp.arange(n)
    row = (t // GRID_W).astype(F32)
    col = (t % GRID_W).astype(F32)
    inv = ROPE_BASE ** (-jnp.arange(half, dtype=F32) / half)
    ar = row[:, None] * inv[None, :]
    ac = col[:, None] * inv[None, :]
    cos = jnp.concatenate([jnp.cos(ar), jnp.cos(ar), jnp.cos(ac), jnp.cos(ac)], axis=1)
    sin = jnp.concatenate([-jnp.sin(ar), jnp.sin(ar), -jnp.sin(ac), jnp.sin(ac)], axis=1)
    rep = LANE // dim
    return jnp.tile(cos, (1, rep)), jnp.tile(sin, (1, rep)), half


def _rope_chunk(yc, cos, sin, half):
    lane = lax.broadcasted_iota(jnp.int32, (1, LANE), 1)
    first = (lane % (2 * half)) < half
    rot = jnp.where(first, pltpu.roll(yc, LANE - half, 1), pltpu.roll(yc, half, 1))
    return yc * cos + rot * sin


def _proj_kernel(*refs, modulate, rope_lo, rope_hi, half):
    refs = list(refs)
    x_ref = refs.pop(0)
    if modulate:
        sh_ref, sc_ref = refs.pop(0), refs.pop(0)
    w_ref = refs.pop(0)
    if rope_hi > rope_lo:
        cos_ref, sin_ref = refs.pop(0), refs.pop(0)
    o_ref = refs.pop(0)
    j = pl.program_id(2)
    if modulate:
        h_ref = refs.pop(0)

        @pl.when(j == 0)
        def _():
            h_ref[...] = (x_ref[0] * (1.0 + sc_ref[0]) + sh_ref[0]).astype(BF16)

        h = h_ref[...]
    else:
        h = x_ref[0]
    y = jnp.dot(h, w_ref[...], preferred_element_type=F32)
    if rope_hi > rope_lo:
        in_rope = (j >= rope_lo) & (j < rope_hi)

        @pl.when(in_rope)
        def _():
            cos, sin = cos_ref[...], sin_ref[...]
            for cidx in range(y.shape[1] // LANE):
                sl = slice(cidx * LANE, (cidx + 1) * LANE)
                o_ref[0, :, sl] = _rope_chunk(y[:, sl], cos, sin, half).astype(o_ref.dtype)

        @pl.when(jnp.logical_not(in_rope))
        def _():
            o_ref[0] = y.astype(o_ref.dtype)
    else:
        o_ref[0] = y.astype(o_ref.dtype)


def _proj(x, w, *, mod=None, ctx=False, rope=None, rope_cols=(0, 0)):
    B, R, K = x.shape
    N = w.shape[1]
    tm = _tile(R, TM)
    g = math.gcd(N, math.gcd(rope_cols[0], rope_cols[1])) if rope is not None else N
    tn = _tile(g, TN, LANE)
    modulate = mod is not None
    args = [x]
    in_specs = [pl.BlockSpec((1, tm, K), lambda b, i, j: (b, i, 0))]
    if modulate:
        shift, scale = mod
        row = _mod_row(ctx, B)
        in_specs += [pl.BlockSpec((1, 1, K), lambda b, i, j: (row(b), 0, 0))] * 2
        args += [shift, scale]
    in_specs.append(pl.BlockSpec((K, tn), lambda b, i, j: (0, j)))
    args.append(w)
    half = 0
    if rope is not None:
        cos, sin, half = rope
        in_specs += [pl.BlockSpec((tm, LANE), lambda b, i, j: (i, 0))] * 2
        args += [cos, sin]
    kern = functools.partial(_proj_kernel, modulate=modulate, rope_lo=rope_cols[0] // tn,
                             rope_hi=rope_cols[1] // tn if rope is not None else 0, half=half)
    return pl.pallas_call(
        kern,
        grid=(B, R // tm, N // tn),
        in_specs=in_specs,
        out_specs=pl.BlockSpec((1, tm, tn), lambda b, i, j: (b, i, j)),
        out_shape=jax.ShapeDtypeStruct((B, R, N), BF16),
        scratch_shapes=[pltpu.VMEM((tm, K), BF16)] if modulate else [],
        compiler_params=_params("parallel", "parallel", "arbitrary"),
        name="proj",
    )(*args)


def _res_ln(x, y, gate, g, b, alpha):
    z = alpha * x + gate * y
    mu = jnp.mean(z, axis=-1, keepdims=True)
    zc = z - mu
    var = jnp.mean(zc * zc, axis=-1, keepdims=True)
    return zc * lax.rsqrt(var + LN_EPS) * g + b


def _oproj_kernel(x_ref, o_ref, w_ref, gate_ref, g_ref, b_ref, out_ref, *, alpha):
    y = jnp.dot(o_ref[0], w_ref[...], preferred_element_type=F32)
    out_ref[0] = _res_ln(x_ref[0], y, gate_ref[0], g_ref[...], b_ref[...], alpha)


def _oproj_ln(x, o, w, gate, g, b, *, ctx, alpha):
    B, R, D = x.shape
    K = o.shape[2]
    tm = _tile(R, TM)
    row = _mod_row(ctx, B)
    return pl.pallas_call(
        functools.partial(_oproj_kernel, alpha=alpha),
        grid=(B, R // tm),
        in_specs=[pl.BlockSpec((1, tm, D), lambda b, i: (b, i, 0)),
                  pl.BlockSpec((1, tm, K), lambda b, i: (b, i, 0)),
                  pl.BlockSpec((K, D), lambda b, i: (0, 0)),
                  pl.BlockSpec((1, 1, D), lambda b, i: (row(b), 0, 0)),
                  pl.BlockSpec((1, D), lambda b, i: (0, 0)),
                  pl.BlockSpec((1, D), lambda b, i: (0, 0))],
        out_specs=pl.BlockSpec((1, tm, D), lambda b, i: (b, i, 0)),
        out_shape=jax.ShapeDtypeStruct((B, R, D), F32),
        compiler_params=_params("parallel", "parallel"),
        name="oproj_ln",
    )(x, o, w, gate, g.reshape(1, D), b.reshape(1, D))


def _flash_init(m_ref, l_ref, acc_ref):
    m_ref[...] = jnp.full(m_ref.shape, NEG_INF, F32)
    l_ref[...] = jnp.zeros(l_ref.shape, F32)
    acc_ref[...] = jnp.zeros(acc_ref.shape, F32)


def _flash_step(s, v, m_ref, l_ref, acc_ref, idx):
    m_prev = m_ref[idx]
    m_new = jnp.maximum(m_prev, jnp.max(s, axis=1, keepdims=True))
    a = jnp.exp(m_prev - m_new)
    p = jnp.exp(s - m_new)
    l_ref[idx] = a * l_ref[idx] + jnp.sum(p, axis=1, keepdims=True)
    acc_ref[idx] = a * acc_ref[idx] + jnp.dot(p.astype(BF16), v, preferred_element_type=F32)
    m_ref[idx] = m_new


def _dot_nt(a, b):
    return lax.dot_general(a, b, (((1,), (1,)), ((), ())), preferred_element_type=F32)


def _diff_attn_kernel(lam_ref, g_ref, q_ref, k_ref, v_ref, o_ref, m_ref, l_ref, acc_ref, *, lam_init, nk):
    ki = pl.program_id(3)

    @pl.when(ki == 0)
    def _():
        _flash_init(m_ref, l_ref, acc_ref)

    q, k, v = q_ref[0], k_ref[0], v_ref[0]
    scale = HEAD ** -0.5
    for c in range(2):
        sl = slice(c * HEAD, (c + 1) * HEAD)
        s = _dot_nt(q[:, sl], k[:, sl]) * scale
        _flash_step(s, v, m_ref, l_ref, acc_ref, c)

    @pl.when(ki == nk - 1)
    def _():
        lam = lam_ref[...]
        lam_full = (jnp.exp(jnp.sum(lam[0:1] * lam[1:2], axis=1, keepdims=True))
                    - jnp.exp(jnp.sum(lam[2:3] * lam[3:4], axis=1, keepdims=True)) + lam_init)
        o = acc_ref[0] / l_ref[0] - lam_full * (acc_ref[1] / l_ref[1])
        ms = jnp.mean(o * o, axis=1, keepdims=True)
        o_ref[0] = (o * lax.rsqrt(ms + RMS_EPS) * g_ref[...] * (1.0 - lam_init)).astype(o_ref.dtype)


def _diff_attn(q_src, kv_src, lam, subln_g, lam_init, H):
    B, nq, _ = q_src.shape
    nkv = kv_src.shape[1]
    W = 2 * HEAD
    tq = _tile(nq, TQ)
    tk = _tile(nkv, TK)
    nk = nkv // tk
    return pl.pallas_call(
        functools.partial(_diff_attn_kernel, lam_init=lam_init, nk=nk),
        grid=(B, H, nq // tq, nk),
        in_specs=[pl.BlockSpec((4, HEAD), lambda b, h, i, k: (0, 0)),
                  pl.BlockSpec((1, W), lambda b, h, i, k: (0, 0)),
                  pl.BlockSpec((1, tq, W), lambda b, h, i, k: (b, i, h)),
                  pl.BlockSpec((1, tk, W), lambda b, h, i, k: (b, k, H + h)),
                  pl.BlockSpec((1, tk, W), lambda b, h, i, k: (b, k, 2 * H + h))],
        out_specs=pl.BlockSpec((1, tq, W), lambda b, h, i, k: (b, i, h)),
        out_shape=jax.ShapeDtypeStruct((B, nq, H * W), BF16),
        scratch_shapes=[pltpu.VMEM((2, tq, 1), F32), pltpu.VMEM((2, tq, 1), F32),
                        pltpu.VMEM((2, tq, W), F32)],
        compiler_params=_params("parallel", "parallel", "parallel", "arbitrary"),
        name="diff_attn",
    )(lam.astype(F32), subln_g.reshape(1, W), q_src, kv_src, kv_src)


def _mla_attn_kernel(qn_ref, qr_ref, kn_ref, kr_ref, v_ref, o_ref, m_ref, l_ref, acc_ref, *, scale, nk):
    ki = pl.program_id(3)

    @pl.when(ki == 0)
    def _():
        _flash_init(m_ref, l_ref, acc_ref)

    q = jnp.concatenate([qn_ref[0], qr_ref[0]], axis=1)
    k = jnp.concatenate([kn_ref[0], kr_ref[0]], axis=1)
    s = _dot_nt(q, k) * scale
    _flash_step(s, v_ref[0], m_ref, l_ref, acc_ref, 0)

    @pl.when(ki == nk - 1)
    def _():
        o_ref[0] = (acc_ref[0] / l_ref[0]).astype(o_ref.dtype)


def _mla_attn(q_src, kv_src, kr_src, H):
    B, nq, _ = q_src.shape
    nkv = kv_src.shape[1]
    tq = _tile(nq, TQ)
    tk = _tile(nkv, TK)
    nk = nkv // tk
    scale = (HEAD + MLA_ROPE) ** -0.5
    return pl.pallas_call(
        functools.partial(_mla_attn_kernel, scale=scale, nk=nk),
        grid=(B, H, nq // tq, nk),
        in_specs=[pl.BlockSpec((1, tq, HEAD), lambda b, h, i, k: (b, i, h)),
                  pl.BlockSpec((1, tq, HEAD), lambda b, h, i, k: (b, i, H + h)),
                  pl.BlockSpec((1, tk, HEAD), lambda b, h, i, k: (b, k, h)),
                  pl.BlockSpec((1, tk, HEAD), lambda b, h, i, k: (b, k, 0)),
                  pl.BlockSpec((1, tk, HEAD), lambda b, h, i, k: (b, k, H + h))],
        out_specs=pl.BlockSpec((1, tq, HEAD), lambda b, h, i, k: (b, i, h)),
        out_shape=jax.ShapeDtypeStruct((B, nq, H * HEAD), BF16),
        scratch_shapes=[pltpu.VMEM((1, tq, 1), F32), pltpu.VMEM((1, tq, 1), F32),
                        pltpu.VMEM((1, tq, HEAD), F32)],
        compiler_params=_params("parallel", "parallel", "parallel", "arbitrary"),
        name="mla_attn",
    )(q_src, q_src, kv_src, kr_src, kv_src)


def _mla_down_kernel(x_ref, sh_ref, sc_ref, w_ref, gq_ref, gkv_ref, *rest, rq, rkv, rope):
    if rope:
        cos_ref, sin_ref, cq_ref, ckv_ref, kr_ref = rest
    else:
        cq_ref, ckv_ref, kr_ref = rest
    h = (x_ref[0] * (1.0 + sc_ref[0]) + sh_ref[0]).astype(BF16)
    y = jnp.dot(h, w_ref[...], preferred_element_type=F32)

    def rms(t, g):
        return t * lax.rsqrt(jnp.mean(t * t, axis=1, keepdims=True) + RMS_EPS) * g

    cq_ref[0] = rms(y[:, :rq], gq_ref[...]).astype(BF16)
    ckv_ref[0] = rms(y[:, rq:rq + rkv], gkv_ref[...]).astype(BF16)
    kr = y[:, rq + rkv:]
    if rope:
        kr = _rope_chunk(kr, cos_ref[...], sin_ref[...], MLA_ROPE // 4)
    kr_ref[0] = kr.astype(BF16)


def _mla_down(x, w, gq, gkv, mod, *, ctx, rope):
    B, R, D = x.shape
    rq, rkv = gq.shape[0], gkv.shape[0]
    N = w.shape[1]
    tm = _tile(R, TM)
    row = _mod_row(ctx, B)
    in_specs = [pl.BlockSpec((1, tm, D), lambda b, i: (b, i, 0)),
                pl.BlockSpec((1, 1, D), lambda b, i: (row(b), 0, 0)),
                pl.BlockSpec((1, 1, D), lambda b, i: (row(b), 0, 0)),
                pl.BlockSpec((D, N), lambda b, i: (0, 0)),
                pl.BlockSpec((1, rq), lambda b, i: (0, 0)),
                pl.BlockSpec((1, rkv), lambda b, i: (0, 0))]
    args = [x, mod[0], mod[1], w, gq.reshape(1, rq), gkv.reshape(1, rkv)]
    if rope is not None:
        in_specs += [pl.BlockSpec((tm, LANE), lambda b, i: (i, 0))] * 2
        args += [rope[0], rope[1]]
    return pl.pallas_call(
        functools.partial(_mla_down_kernel, rq=rq, rkv=rkv, rope=rope is not None),
        grid=(B, R // tm),
        in_specs=in_specs,
        out_specs=[pl.BlockSpec((1, tm, rq), lambda b, i: (b, i, 0)),
                   pl.BlockSpec((1, tm, rkv), lambda b, i: (b, i, 0)),
                   pl.BlockSpec((1, tm, LANE), lambda b, i: (b, i, 0))],
        out_shape=[jax.ShapeDtypeStruct((B, R, rq), BF16),
                   jax.ShapeDtypeStruct((B, R, rkv), BF16),
                   jax.ShapeDtypeStruct((B, R, LANE), BF16)],
        compiler_params=_params("parallel", "parallel"),
        name="mla_down",
    )(*args)


def _sink_softmax_pv(s, sk, v):
    mx = jnp.maximum(jnp.max(s, axis=1, keepdims=True), sk)
    e = jnp.exp(s - mx)
    r = 1.0 / (jnp.sum(e, axis=1, keepdims=True) + jnp.exp(sk - mx))
    return jnp.dot((e * r).astype(BF16), v, preferred_element_type=F32)


def _win_attn_kernel(sink_ref, q_ref, kp_ref, kc_ref, kn_ref, kx_ref, vp_ref, vc_ref, vn_ref, vx_ref, o_ref,
                     *, tq, n, R):
    g = pl.program_id(1)
    qi = pl.program_id(2)
    k = jnp.concatenate([kp_ref[0], kc_ref[0], kn_ref[0], kx_ref[0]], axis=0)
    v = jnp.concatenate([vp_ref[0], vc_ref[0], vn_ref[0], vx_ref[0]], axis=0)
    L = k.shape[0]
    q_pos = qi * tq + lax.broadcasted_iota(jnp.int32, (tq, L), 0)
    col = lax.broadcasted_iota(jnp.int32, (tq, L), 1)
    key_pos = (qi - 1) * tq + col
    valid = ((jnp.abs(q_pos - key_pos) <= WINDOW) & (key_pos >= 0) & (key_pos < n)) | (col >= 3 * tq)
    q = q_ref[0]
    scale = HEAD ** -0.5
    for r in range(R):
        sl = slice(r * HEAD, (r + 1) * HEAD)
        s = jnp.where(valid, _dot_nt(q[:, sl], k) * scale, NEG_INF)
        o_ref[0, :, sl] = _sink_softmax_pv(s, sink_ref[g * R + r], v).astype(o_ref.dtype)


def _win_attn(qkv, qkv_ctx, sink, H, G):
    B, n, _ = qkv.shape
    m = qkv_ctx.shape[1]
    R = H // G
    tq = _tile(n, WINDOW)
    assert tq == WINDOW, "window kernel walks previous/current/next key blocks of WINDOW rows"
    nb = n // tq
    prev = lambda i: jnp.maximum(i - 1, 0)
    nxt = lambda i: jnp.minimum(i + 1, nb - 1)
    kspec = lambda f, off: pl.BlockSpec((1, tq, HEAD), lambda b, g, i: (b, f(i), off + g))
    xspec = lambda off: pl.BlockSpec((1, m, HEAD), lambda b, g, i: (b, 0, off + g))
    ident = lambda i: i
    return pl.pallas_call(
        functools.partial(_win_attn_kernel, tq=tq, n=n, R=R),
        grid=(B, G, nb),
        in_specs=[pl.BlockSpec(memory_space=pltpu.SMEM),
                  pl.BlockSpec((1, tq, R * HEAD), lambda b, g, i: (b, i, g)),
                  kspec(prev, H), kspec(ident, H), kspec(nxt, H), xspec(H),
                  kspec(prev, H + G), kspec(ident, H + G), kspec(nxt, H + G), xspec(H + G)],
        out_specs=pl.BlockSpec((1, tq, R * HEAD), lambda b, g, i: (b, i, g)),
        out_shape=jax.ShapeDtypeStruct((B, n, H * HEAD), BF16),
        compiler_params=_params("parallel", "parallel", "parallel"),
        name="win_attn",
    )(sink.astype(F32), qkv, qkv, qkv, qkv, qkv_ctx, qkv, qkv, qkv, qkv_ctx)


def _sink_attn_kernel(sink_ref, q_ref, k_ref, v_ref, o_ref, *, R):
    g = pl.program_id(1)
    q, k, v = q_ref[0], k_ref[0], v_ref[0]
    scale = HEAD ** -0.5
    for r in range(R):
        sl = slice(r * HEAD, (r + 1) * HEAD)
        s = _dot_nt(q[:, sl], k) * scale
        o_ref[0, :, sl] = _sink_softmax_pv(s, sink_ref[g * R + r], v).astype(o_ref.dtype)


def _sink_attn(qkv, sink, H, G):
    B, m, _ = qkv.shape
    R = H // G
    return pl.pallas_call(
        functools.partial(_sink_attn_kernel, R=R),
        grid=(B, G),
        in_specs=[pl.BlockSpec(memory_space=pltpu.SMEM),
                  pl.BlockSpec((1, m, R * HEAD), lambda b, g: (b, 0, g)),
                  pl.BlockSpec((1, m, HEAD), lambda b, g: (b, 0, H + g)),
                  pl.BlockSpec((1, m, HEAD), lambda b, g: (b, 0, H + G + g))],
        out_specs=pl.BlockSpec((1, m, R * HEAD), lambda b, g: (b, 0, g)),
        out_shape=jax.ShapeDtypeStruct((B, m, H * HEAD), BF16),
        compiler_params=_params("parallel", "parallel"),
        name="sink_attn",
    )(sink.astype(F32), qkv, qkv, qkv)


def _silu(g):
    return g * jax.nn.sigmoid(g)


def _ffn_kernel(x_ref, sh_ref, sc_ref, gate_ref, w1_ref, w3_ref, w2_ref, g_ref, b_ref, o_ref, h_ref, acc_ref,
                *, alpha, nf):
    f = pl.program_id(2)

    @pl.when(f == 0)
    def _():
        h_ref[...] = (x_ref[0] * (1.0 + sc_ref[0]) + sh_ref[0]).astype(BF16)
        acc_ref[...] = jnp.zeros(acc_ref.shape, F32)

    h = h_ref[...]
    a = _silu(jnp.dot(h, w1_ref[...], preferred_element_type=F32)) * jnp.dot(h, w3_ref[...], preferred_element_type=F32)
    acc_ref[...] += jnp.dot(a.astype(BF16), w2_ref[...], preferred_element_type=F32)

    @pl.when(f == nf - 1)
    def _():
        o_ref[0] = _res_ln(x_ref[0], acc_ref[...], gate_ref[0], g_ref[...], b_ref[...], alpha)


def _ffn(x, w13, w2, mod, g, b, *, ctx, alpha):
    B, R, D = x.shape
    Fd = w2.shape[0]
    tm = _tile(R, TM)
    tf = _tile(Fd, TF, LANE)
    nf = Fd // tf
    row = _mod_row(ctx, B)
    vec = pl.BlockSpec((1, 1, D), lambda b, i, f: (row(b), 0, 0))
    return pl.pallas_call(
        functools.partial(_ffn_kernel, alpha=alpha, nf=nf),
        grid=(B, R // tm, nf),
        in_specs=[pl.BlockSpec((1, tm, D), lambda b, i, f: (b, i, 0)), vec, vec, vec,
                  pl.BlockSpec((D, tf), lambda b, i, f: (0, f)),
                  pl.BlockSpec((D, tf), lambda b, i, f: (0, nf + f)),
                  pl.BlockSpec((tf, D), lambda b, i, f: (f, 0)),
                  pl.BlockSpec((1, D), lambda b, i, f: (0, 0)),
                  pl.BlockSpec((1, D), lambda b, i, f: (0, 0))],
        out_specs=pl.BlockSpec((1, tm, D), lambda b, i, f: (b, i, 0)),
        out_shape=jax.ShapeDtypeStruct((B, R, D), F32),
        scratch_shapes=[pltpu.VMEM((tm, D), BF16), pltpu.VMEM((tm, D), F32)],
        compiler_params=_params("parallel", "parallel", "arbitrary"),
        name="ffn",
    )(x, mod[0], mod[1], mod[2], w13, w13, w2, g.reshape(1, D), b.reshape(1, D))


def _top2_gates(logits):
    E = logits.shape[1]
    lane = lax.broadcasted_iota(jnp.int32, logits.shape, 1)
    m1 = jnp.max(logits, axis=1, keepdims=True)
    i1 = jnp.min(jnp.where(logits == m1, lane, E), axis=1, keepdims=True)
    first = lane == i1
    rest = jnp.where(first, -jnp.inf, logits)
    m2 = jnp.max(rest, axis=1, keepdims=True)
    i2 = jnp.min(jnp.where(rest == m2, lane, E), axis=1, keepdims=True)
    w1 = 1.0 / (1.0 + jnp.exp(m2 - m1))
    return jnp.where(first, w1, 0.0) + jnp.where(lane == i2, 1.0 - w1, 0.0)


def _moe_dense_kernel(x_ref, sh_ref, sc_ref, gate_ref, rt_ref, w1_ref, w3_ref, w2_ref, g_ref, b_ref, o_ref,
                      h_ref, gates_ref, acce_ref, acc_ref, *, alpha, ne, nf):
    e = pl.program_id(2)
    f = pl.program_id(3)

    @pl.when((e == 0) & (f == 0))
    def _():
        hf = x_ref[0] * (1.0 + sc_ref[0]) + sh_ref[0]
        h_ref[...] = hf.astype(BF16)
        logits = jnp.dot(hf, rt_ref[...], preferred_element_type=F32, precision=lax.Precision.HIGHEST)
        gates_ref[...] = _top2_gates(logits)
        acc_ref[...] = jnp.zeros(acc_ref.shape, F32)

    @pl.when(f == 0)
    def _():
        acce_ref[...] = jnp.zeros(acce_ref.shape, F32)

    h = h_ref[...]
    a = _silu(jnp.dot(h, w1_ref[0], preferred_element_type=F32)) * jnp.dot(h, w3_ref[0], preferred_element_type=F32)
    acce_ref[...] += jnp.dot(a.astype(BF16), w2_ref[0], preferred_element_type=F32)

    @pl.when(f == nf - 1)
    def _():
        gates = gates_ref[...]
        lane = lax.broadcasted_iota(jnp.int32, gates.shape, 1)
        ge = jnp.sum(jnp.where(lane == e, gates, 0.0), axis=1, keepdims=True)
        acc_ref[...] += ge * acce_ref[...]

    @pl.when((e == ne - 1) & (f == nf - 1))
    def _():
        o_ref[0] = _res_ln(x_ref[0], acc_ref[...], gate_ref[0], g_ref[...], b_ref[...], alpha)


def _moe_dense(x, router, w13, w2, mod, g, b, *, ctx, alpha):
    B, R, D = x.shape
    E, Fd, _ = w2.shape
    tm = _tile(R, TM)
    tf = _tile(Fd, TF, LANE)
    nf = Fd // tf
    row = _mod_row(ctx, B)
    vec = pl.BlockSpec((1, 1, D), lambda b, i, e, f: (row(b), 0, 0))
    return pl.pallas_call(
        functools.partial(_moe_dense_kernel, alpha=alpha, ne=E, nf=nf),
        grid=(B, R // tm, E, nf),
        in_specs=[pl.BlockSpec((1, tm, D), lambda b, i, e, f: (b, i, 0)), vec, vec, vec,
                  pl.BlockSpec((D, E), lambda b, i, e, f: (0, 0)),
                  pl.BlockSpec((1, D, tf), lambda b, i, e, f: (e, 0, f)),
                  pl.BlockSpec((1, D, tf), lambda b, i, e, f: (e, 0, nf + f)),
                  pl.BlockSpec((1, tf, D), lambda b, i, e, f: (e, f, 0)),
                  pl.BlockSpec((1, D), lambda b, i, e, f: (0, 0)),
                  pl.BlockSpec((1, D), lambda b, i, e, f: (0, 0))],
        out_specs=pl.BlockSpec((1, tm, D), lambda b, i, e, f: (b, i, 0)),
        out_shape=jax.ShapeDtypeStruct((B, R, D), F32),
        scratch_shapes=[pltpu.VMEM((tm, D), BF16), pltpu.VMEM((tm, E), F32),
                        pltpu.VMEM((tm, D), F32), pltpu.VMEM((tm, D), F32)],
        compiler_params=_params("parallel", "parallel", "arbitrary", "arbitrary"),
        name="moe_dense",
    )(x, mod[0], mod[1], mod[2], router, w13, w13, w2, g.reshape(1, D), b.reshape(1, D))


def _mix_diff(xl, xc, mod, need_ctx, layer, wqkv, lam, subln_g, rope128):
    D = xl.shape[2]
    H = D // (2 * HEAD)
    lam_init = 0.8 - 0.6 * math.exp(-0.3 * layer)
    w = wqkv.astype(BF16)
    qkv_l = _proj(xl, w, mod=mod, ctx=False, rope=rope128, rope_cols=(0, 2 * D))
    qkv_c = _proj(xc, w, mod=mod, ctx=True)
    kv_all = jnp.concatenate([qkv_l, qkv_c], axis=1)
    o_l = _diff_attn(qkv_l, kv_all, lam, subln_g, lam_init, H)
    o_c = _diff_attn(qkv_c, qkv_c, lam, subln_g, lam_init, H) if need_ctx else None
    return o_l, o_c


def _mix_window(xl, xc, mod, need_ctx, wqkv, sink, rope128):
    D = xl.shape[2]
    H = D // HEAD
    G = H // 4
    w = wqkv.astype(BF16)
    qkv_l = _proj(xl, w, mod=mod, ctx=False, rope=rope128, rope_cols=(0, (H + G) * HEAD))
    qkv_c = _proj(xc, w, mod=mod, ctx=True)
    o_l = _win_attn(qkv_l, qkv_c, sink, H, G)
    o_c = _sink_attn(qkv_c, sink, H, G) if need_ctx else None
    return o_l, o_c


def _mix_mla(xl, xc, mod, need_ctx, w_down, gq, gkv, w_uq, w_ukv, rope64):
    D = xl.shape[2]
    H = D // HEAD
    rq, rkv = gq.shape[0], gkv.shape[0]
    wd = jnp.pad(w_down, ((0, 0), (0, LANE - MLA_ROPE))).astype(BF16)
    wq3 = w_uq.reshape(rq, H, HEAD + MLA_ROPE)
    wq = jnp.concatenate(
        [wq3[:, :, :HEAD].reshape(rq, H * HEAD),
         jnp.pad(wq3[:, :, HEAD:], ((0, 0), (0, 0), (0, HEAD - MLA_ROPE))).reshape(rq, H * HEAD)], axis=1).astype(BF16)
    wkv3 = w_ukv.reshape(rkv, H, 2 * HEAD)
    wkv = jnp.concatenate([wkv3[:, :, :HEAD].reshape(rkv, H * HEAD),
                           wkv3[:, :, HEAD:].reshape(rkv, H * HEAD)], axis=1).astype(BF16)
    cq_l, ckv_l, kr_l = _mla_down(xl, wd, gq, gkv, mod, ctx=False, rope=rope64)
    cq_c, ckv_c, kr_c = _mla_down(xc, wd, gq, gkv, mod, ctx=True, rope=None)
    q_l = _proj(cq_l, wq, rope=rope64, rope_cols=(H * HEAD, 2 * H * HEAD))
    kv_l = _proj(ckv_l, wkv)
    kv_c = _proj(ckv_c, wkv)
    kv_all = jnp.concatenate([kv_l, kv_c], axis=1)
    kr_all = jnp.concatenate([kr_l, kr_c], axis=1)
    o_l = _mla_attn(q_l, kv_all, kr_all, H)
    o_c = None
    if need_ctx:
        q_c = _proj(cq_c, wq)
        o_c = _mla_attn(q_c, kv_c, kr_c, H)
    return o_l, o_c


def kernel(x, c, ctx, c_ctx, mod_w, mod_b, ln1_g, ln1_b, ln2_g, ln2_b, da_wqkv, da_lambda, da_subln_g, da_wo, wa_wqkv, wa_sink, wa_wo, mla_w_down, mla_q_norm_g, mla_kv_norm_g, mla_w_uq, mla_w_ukv, mla_wo, ffn_w13, ffn_w2, moe_router, moe_w13, moe_w2):
    B, n, D = x.shape
    depth = mod_w.shape[0]
    alpha = (2.0 * depth) ** 0.25
    rows = SUBLANE * (-(-(B + 1) // SUBLANE))
    cc = jnp.concatenate([c, c_ctx[None, :], jnp.zeros((rows - B - 1, D), F32)], axis=0)
    mod_all = _modulation(cc, mod_w, mod_b)
    rope128 = _rope_tables(n, HEAD)
    rope64 = _rope_tables(n, MLA_ROPE)
    xl, xc = x, ctx
    for i in range(depth):
        need_ctx = i < depth - 1
        mod = [mod_all[i, :, k * D:(k + 1) * D].reshape(rows, 1, D) for k in range(6)]
        kind, slot = i % N_MIXERS, i // N_MIXERS
        if kind == 0:
            o_l, o_c = _mix_diff(xl, xc, mod[0:2], need_ctx, i, da_wqkv[slot], da_lambda[slot],
                                 da_subln_g[slot], rope128)
            wo = da_wo[slot]
        elif kind == 1:
            o_l, o_c = _mix_window(xl, xc, mod[0:2], need_ctx, wa_wqkv[slot], wa_sink[slot], rope128)
            wo = wa_wo[slot]
        else:
            o_l, o_c = _mix_mla(xl, xc, mod[0:2], need_ctx, mla_w_down[slot], mla_q_norm_g[slot],
                                mla_kv_norm_g[slot], mla_w_uq[slot], mla_w_ukv[slot], rope64)
            wo = mla_wo[slot]
        wo = wo.astype(BF16)
        xl = _oproj_ln(xl, o_l, wo, mod[2], ln1_g[i], ln1_b[i], ctx=False, alpha=alpha)
        if need_ctx:
            xc = _oproj_ln(xc, o_c, wo, mod[2], ln1_g[i], ln1_b[i], ctx=True, alpha=alpha)
        cslot = i // 2
        if i % 2 == 0:
            w13, w2 = ffn_w13[cslot].astype(BF16), ffn_w2[cslot].astype(BF16)
            xl = _ffn(xl, w13, w2, mod[3:6], ln2_g[i], ln2_b[i], ctx=False, alpha=alpha)
            if need_ctx:
                xc = _ffn(xc, w13, w2, mod[3:6], ln2_g[i], ln2_b[i], ctx=True, alpha=alpha)
        else:
            w13, w2 = moe_w13[cslot].astype(BF16), moe_w2[cslot].astype(BF16)
            xl = _moe_dense(xl, moe_router[cslot], w13, w2, mod[3:6], ln2_g[i], ln2_b[i], ctx=False, alpha=alpha)
            if need_ctx:
                xc = _moe_dense(xc, moe_router[cslot], w13, w2, mod[3:6], ln2_g[i], ln2_b[i], ctx=True, alpha=alpha)
    return xl
```

```python
import functools
import math

import jax
import jax.numpy as jnp
from jax import lax
from jax.experimental import pallas as pl
from jax.experimental.pallas import tpu as pltpu

F32 = jnp.float32
BF16 = jnp.bfloat16

LANE = 128
SUBLANE = 8
VMEM_LIMIT = 56 * 1024 * 1024

GRID_W = 64
ROPE_BASE = 10000.0
LN_EPS = 1e-5
RMS_EPS = 1e-6
NEG_INF = -1e30
HEAD = 128
WINDOW = 128
MLA_ROPE = 64
N_MIXERS = 3

TM = 512
TN = 1024
TF = 512
TQ = 1024
TK = 2816
RB = 256
TME = 1024
ROW_UNROLL = 8


def _tile(dim, pref, align=SUBLANE):
    if dim <= pref:
        return dim
    t = (pref // align) * align
    while t >= align:
        if dim % t == 0:
            return t
        t -= align
    return dim


def _params(*sem):
    return pltpu.CompilerParams(dimension_semantics=sem, vmem_limit_bytes=VMEM_LIMIT)


def _mod_row(ctx, nb):
    return (lambda b: nb) if ctx else (lambda b: b)


def _mod_kernel(s_ref, w_ref, b_ref, o_ref):
    s = s_ref[...]
    s = s * jax.nn.sigmoid(s)
    o_ref[0] = jnp.dot(s.astype(BF16), w_ref[0].astype(BF16), preferred_element_type=F32) + b_ref[0]


def _modulation(cc, mod_w, mod_b):
    depth, D, N = mod_w.shape
    R = cc.shape[0]
    tn = _tile(N, 1024, LANE)
    return pl.pallas_call(
        _mod_kernel,
        grid=(depth, N // tn),
        in_specs=[pl.BlockSpec((R, D), lambda l, j: (0, 0)),
                  pl.BlockSpec((1, D, tn), lambda l, j: (l, 0, j)),
                  pl.BlockSpec((1, 1, tn), lambda l, j: (l, 0, j))],
        out_specs=pl.BlockSpec((1, R, tn), lambda l, j: (l, 0, j)),
        out_shape=jax.ShapeDtypeStruct((depth, R, N), F32),
        compiler_params=_params("parallel", "parallel"),
        name="modulation",
    )(cc, mod_w, mod_b.reshape(depth, 1, N))


def _rope_tables(n, dim):
    half = dim // 4
    t = jnp.arange(n)
    row = (t // GRID_W).astype(F32)
    col = (t % GRID_W).astype(F32)
    inv = ROPE_BASE ** (-jnp.arange(half, dtype=F32) / half)
    ar = row[:, None] * inv[None, :]
    ac = col[:, None] * inv[None, :]
    cos = jnp.concatenate([jnp.cos(ar), jnp.cos(ar), jnp.cos(ac), jnp.cos(ac)], axis=1)
    sin = jnp.concatenate([-jnp.sin(ar), jnp.sin(ar), -jnp.sin(ac), jnp.sin(ac)], axis=1)
    rep = LANE // dim
    return jnp.tile(cos, (1, rep)), jnp.tile(sin, (1, rep)), half


def _rope_chunk(yc, cos, sin, half):
    lane = lax.broadcasted_iota(jnp.int32, (1, LANE), 1)
    first = (lane % (2 * half)) < half
    rot = jnp.where(first, pltpu.roll(yc, LANE - half, 1), pltpu.roll(yc, half, 1))
    return yc * cos + rot * sin


def _proj_kernel(*refs, modulate, rope_lo, rope_hi, half):
    refs = list(refs)
    x_ref = refs.pop(0)
    if modulate:
        sh_ref, sc_ref = refs.pop(0), refs.pop(0)
    w_ref = refs.pop(0)
    if rope_hi > rope_lo:
        cos_ref, sin_ref = refs.pop(0), refs.pop(0)
    o_ref = refs.pop(0)
    j = pl.program_id(2)
    if modulate:
        h_ref = refs.pop(0)

        @pl.when(j == 0)
        def _():
            h_ref[...] = (x_ref[0] * (1.0 + sc_ref[0]) + sh_ref[0]).astype(BF16)

        h = h_ref[...]
    else:
        h = x_ref[0]
    y = jnp.dot(h, w_ref[...], preferred_element_type=F32)
    if rope_hi > rope_lo:
        in_rope = (j >= rope_lo) & (j < rope_hi)

        @pl.when(in_rope)
        def _():
            cos, sin = cos_ref[...], sin_ref[...]
            for cidx in range(y.shape[1] // LANE):
                sl = slice(cidx * LANE, (cidx + 1) * LANE)
                o_ref[0, :, sl] = _rope_chunk(y[:, sl], cos, sin, half).astype(o_ref.dtype)

        @pl.when(jnp.logical_not(in_rope))
        def _():
            o_ref[0] = y.astype(o_ref.dtype)
    else:
        o_ref[0] = y.astype(o_ref.dtype)


def _proj(x, w, *, mod=None, ctx=False, rope=None, rope_cols=(0, 0)):
    B, R, K = x.shape
    N = w.shape[1]
    tm = _tile(R, TM)
    g = math.gcd(N, math.gcd(rope_cols[0], rope_cols[1])) if rope is not None else N
    tn = _tile(g, TN, LANE)
    modulate = mod is not None
    args = [x]
    in_specs = [pl.BlockSpec((1, tm, K), lambda b, i, j: (b, i, 0))]
    if modulate:
        shift, scale = mod
        row = _mod_row(ctx, B)
        in_specs += [pl.BlockSpec((1, 1, K), lambda b, i, j: (row(b), 0, 0))] * 2
        args += [shift, scale]
    in_specs.append(pl.BlockSpec((K, tn), lambda b, i, j: (0, j)))
    args.append(w)
    half = 0
    if rope is not None:
        cos, sin, half = rope
        in_specs += [pl.BlockSpec((tm, LANE), lambda b, i, j: (i, 0))] * 2
        args += [cos, sin]
    kern = functools.partial(_proj_kernel, modulate=modulate, rope_lo=rope_cols[0] // tn,
                             rope_hi=rope_cols[1] // tn if rope is not None else 0, half=half)
    return pl.pallas_call(
        kern,
        grid=(B, R // tm, N // tn),
        in_specs=in_specs,
        out_specs=pl.BlockSpec((1, tm, tn), lambda b, i, j: (b, i, j)),
        out_shape=jax.ShapeDtypeStruct((B, R, N), BF16),
        scratch_shapes=[pltpu.VMEM((tm, K), BF16)] if modulate else [],
        compiler_params=_params("parallel", "parallel", "arbitrary"),
        name="proj",
    )(*args)


def _res_ln(x, y, gate, g, b, alpha):
    z = alpha * x + gate * y
    mu = jnp.mean(z, axis=-1, keepdims=True)
    zc = z - mu
    var = jnp.mean(zc * zc, axis=-1, keepdims=True)
    return zc * lax.rsqrt(var + LN_EPS) * g + b


def _oproj_kernel(x_ref, o_ref, w_ref, gate_ref, g_ref, b_ref, out_ref, *, alpha):
    y = jnp.dot(o_ref[0], w_ref[...], preferred_element_type=F32)
    out_ref[0] = _res_ln(x_ref[0], y, gate_ref[0], g_ref[...], b_ref[...], alpha)


def _oproj_ln(x, o, w, gate, g, b, *, ctx, alpha):
    B, R, D = x.shape
    K = o.shape[2]
    tm = _tile(R, TM)
    row = _mod_row(ctx, B)
    return pl.pallas_call(
        functools.partial(_oproj_kernel, alpha=alpha),
        grid=(B, R // tm),
        in_specs=[pl.BlockSpec((1, tm, D), lambda b, i: (b, i, 0)),
                  pl.BlockSpec((1, tm, K), lambda b, i: (b, i, 0)),
                  pl.BlockSpec((K, D), lambda b, i: (0, 0)),
                  pl.BlockSpec((1, 1, D), lambda b, i: (row(b), 0, 0)),
                  pl.BlockSpec((1, D), lambda b, i: (0, 0)),
                  pl.BlockSpec((1, D), lambda b, i: (0, 0))],
        out_specs=pl.BlockSpec((1, tm, D), lambda b, i: (b, i, 0)),
        out_shape=jax.ShapeDtypeStruct((B, R, D), F32),
        compiler_params=_params("parallel", "parallel"),
        name="oproj_ln",
    )(x, o, w, gate, g.reshape(1, D), b.reshape(1, D))


def _flash_init(m_ref, l_ref, acc_ref):
    m_ref[...] = jnp.full(m_ref.shape, NEG_INF, F32)
    l_ref[...] = jnp.zeros(l_ref.shape, F32)
    acc_ref[...] = jnp.zeros(acc_ref.shape, F32)


def _dot_nt(a, b):
    return lax.dot_general(a, b, (((1,), (1,)), ((), ())), preferred_element_type=F32)


def _flash_step(q_fn, k_fn, v_fn, m_ref, l_ref, acc_ref, *, chains, tq, scale):
    rb = min(RB, tq)
    c_exp = scale * math.log2(math.e)

    def body(r, carry):
        rows = pl.ds(pl.multiple_of(r * rb, rb), rb)
        for c in range(chains):
            s = _dot_nt(q_fn(c, rows), k_fn(c))
            m_prev = m_ref[c, rows]
            m_new = jnp.maximum(m_prev, jnp.max(s, axis=1, keepdims=True))
            a = jnp.exp2((m_prev - m_new) * c_exp)
            p = jnp.exp2((s - m_new) * c_exp)
            l_ref[c, rows] = a * l_ref[c, rows] + jnp.sum(p, axis=1, keepdims=True)
            acc_ref[c, rows] = a * acc_ref[c, rows] + jnp.dot(p.astype(BF16), v_fn(c), preferred_element_type=F32)
            m_ref[c, rows] = m_new
        return carry

    lax.fori_loop(0, tq // rb, body, 0, unroll=4)


def _diff_attn_kernel(lam_ref, g_ref, q_ref, k_ref, v_ref, o_ref, m_ref, l_ref, acc_ref, *, lam_init, nk):
    ki = pl.program_id(3)

    @pl.when(ki == 0)
    def _():
        _flash_init(m_ref, l_ref, acc_ref)

    _flash_step(lambda c, rows: q_ref[0, rows, c * HEAD:(c + 1) * HEAD],
                lambda c: k_ref[0, :, c * HEAD:(c + 1) * HEAD],
                lambda c: v_ref[0],
                m_ref, l_ref, acc_ref, chains=2, tq=q_ref.shape[1], scale=HEAD ** -0.5)

    @pl.when(ki == nk - 1)
    def _():
        lam = lam_ref[...]
        lam_full = (jnp.exp(jnp.sum(lam[0:1] * lam[1:2], axis=1, keepdims=True))
                    - jnp.exp(jnp.sum(lam[2:3] * lam[3:4], axis=1, keepdims=True)) + lam_init)
        o = acc_ref[0] / l_ref[0] - lam_full * (acc_ref[1] / l_ref[1])
        ms = jnp.mean(o * o, axis=1, keepdims=True)
        o_ref[0] = (o * lax.rsqrt(ms + RMS_EPS) * g_ref[...] * (1.0 - lam_init)).astype(o_ref.dtype)


def _diff_attn(q_src, kv_src, lam, subln_g, lam_init, H):
    B, nq, _ = q_src.shape
    nkv = kv_src.shape[1]
    W = 2 * HEAD
    tq = _tile(nq, TQ)
    tk = _tile(nkv, TK)
    nk = nkv // tk
    return pl.pallas_call(
        functools.partial(_diff_attn_kernel, lam_init=lam_init, nk=nk),
        grid=(B, H, nq // tq, nk),
        in_specs=[pl.BlockSpec((4, HEAD), lambda b, h, i, k: (0, 0)),
                  pl.BlockSpec((1, W), lambda b, h, i, k: (0, 0)),
                  pl.BlockSpec((1, tq, W), lambda b, h, i, k: (b, i, h)),
                  pl.BlockSpec((1, tk, W), lambda b, h, i, k: (b, k, H + h)),
                  pl.BlockSpec((1, tk, W), lambda b, h, i, k: (b, k, 2 * H + h))],
        out_specs=pl.BlockSpec((1, tq, W), lambda b, h, i, k: (b, i, h)),
        out_shape=jax.ShapeDtypeStruct((B, nq, H * W), BF16),
        scratch_shapes=[pltpu.VMEM((2, tq, 1), F32), pltpu.VMEM((2, tq, 1), F32),
                        pltpu.VMEM((2, tq, W), F32)],
        compiler_params=_params("parallel", "parallel", "parallel", "arbitrary"),
        name="diff_attn",
    )(lam.astype(F32), subln_g.reshape(1, W), q_src, kv_src, kv_src)


def _mla_attn_kernel(qn_ref, qr_ref, kn_ref, kr_ref, v_ref, o_ref, m_ref, l_ref, acc_ref, k_ref, *, scale, nk):
    ki = pl.program_id(3)

    @pl.when(ki == 0)
    def _():
        _flash_init(m_ref, l_ref, acc_ref)

    for c in range(2):
        k_ref[c, :, :HEAD] = kn_ref[0, :, c * HEAD:(c + 1) * HEAD]
        k_ref[c, :, HEAD:] = kr_ref[0]

    def q_fn(c, rows):
        sl = slice(c * HEAD, (c + 1) * HEAD)
        return jnp.concatenate([qn_ref[0, rows, sl], qr_ref[0, rows, sl]], axis=1)

    _flash_step(q_fn, lambda c: k_ref[c], lambda c: v_ref[0, :, c * HEAD:(c + 1) * HEAD],
                m_ref, l_ref, acc_ref, chains=2, tq=qn_ref.shape[1], scale=scale)

    @pl.when(ki == nk - 1)
    def _():
        for c in range(2):
            o_ref[0, :, c * HEAD:(c + 1) * HEAD] = (acc_ref[c] / l_ref[c]).astype(o_ref.dtype)


def _mla_attn(q_src, kv_src, kr_src, H):
    B, nq, _ = q_src.shape
    nkv = kv_src.shape[1]
    assert H % 2 == 0
    P = H // 2
    W = 2 * HEAD
    tq = _tile(nq, TQ)
    tk = _tile(nkv, TK)
    nk = nkv // tk
    scale = (HEAD + MLA_ROPE) ** -0.5
    return pl.pallas_call(
        functools.partial(_mla_attn_kernel, scale=scale, nk=nk),
        grid=(B, P, nq // tq, nk),
        in_specs=[pl.BlockSpec((1, tq, W), lambda b, h, i, k: (b, i, h)),
                  pl.BlockSpec((1, tq, W), lambda b, h, i, k: (b, i, P + h)),
                  pl.BlockSpec((1, tk, W), lambda b, h, i, k: (b, k, h)),
                  pl.BlockSpec((1, tk, HEAD), lambda b, h, i, k: (b, k, 0)),
                  pl.BlockSpec((1, tk, W), lambda b, h, i, k: (b, k, P + h))],
        out_specs=pl.BlockSpec((1, tq, W), lambda b, h, i, k: (b, i, h)),
        out_shape=jax.ShapeDtypeStruct((B, nq, H * HEAD), BF16),
        scratch_shapes=[pltpu.VMEM((2, tq, 1), F32), pltpu.VMEM((2, tq, 1), F32),
                        pltpu.VMEM((2, tq, HEAD), F32), pltpu.VMEM((2, tk, W), BF16)],
        compiler_params=_params("parallel", "parallel", "parallel", "arbitrary"),
        name="mla_attn",
    )(q_src, q_src, kv_src, kr_src, kv_src)


def _mla_down_kernel(x_ref, sh_ref, sc_ref, w_ref, gq_ref, gkv_ref, *rest, rq, rkv, rope):
    if rope:
        cos_ref, sin_ref, cq_ref, ckv_ref, kr_ref = rest
    else:
        cq_ref, ckv_ref, kr_ref = rest
    h = (x_ref[0] * (1.0 + sc_ref[0]) + sh_ref[0]).astype(BF16)
    y = jnp.dot(h, w_ref[...], preferred_element_type=F32)

    def rms(t, g):
        return t * lax.rsqrt(jnp.mean(t * t, axis=1, keepdims=True) + RMS_EPS) * g

    cq_ref[0] = rms(y[:, :rq], gq_ref[...]).astype(BF16)
    ckv_ref[0] = rms(y[:, rq:rq + rkv], gkv_ref[...]).astype(BF16)
    kr = y[:, rq + rkv:]
    if rope:
        kr = _rope_chunk(kr, cos_ref[...], sin_ref[...], MLA_ROPE // 4)
    kr_ref[0] = kr.astype(BF16)


def _mla_down(x, w, gq, gkv, mod, *, ctx, rope):
    B, R, D = x.shape
    rq, rkv = gq.shape[0], gkv.shape[0]
    N = w.shape[1]
    tm = _tile(R, TM)
    row = _mod_row(ctx, B)
    in_specs = [pl.BlockSpec((1, tm, D), lambda b, i: (b, i, 0)),
                pl.BlockSpec((1, 1, D), lambda b, i: (row(b), 0, 0)),
                pl.BlockSpec((1, 1, D), lambda b, i: (row(b), 0, 0)),
                pl.BlockSpec((D, N), lambda b, i: (0, 0)),
                pl.BlockSpec((1, rq), lambda b, i: (0, 0)),
                pl.BlockSpec((1, rkv), lambda b, i: (0, 0))]
    args = [x, mod[0], mod[1], w, gq.reshape(1, rq), gkv.reshape(1, rkv)]
    if rope is not None:
        in_specs += [pl.BlockSpec((tm, LANE), lambda b, i: (i, 0))] * 2
        args += [rope[0], rope[1]]
    return pl.pallas_call(
        functools.partial(_mla_down_kernel, rq=rq, rkv=rkv, rope=rope is not None),
        grid=(B, R // tm),
        in_specs=in_specs,
        out_specs=[pl.BlockSpec((1, tm, rq), lambda b, i: (b, i, 0)),
                   pl.BlockSpec((1, tm, rkv), lambda b, i: (b, i, 0)),
                   pl.BlockSpec((1, tm, LANE), lambda b, i: (b, i, 0))],
        out_shape=[jax.ShapeDtypeStruct((B, R, rq), BF16),
                   jax.ShapeDtypeStruct((B, R, rkv), BF16),
                   jax.ShapeDtypeStruct((B, R, LANE), BF16)],
        compiler_params=_params("parallel", "parallel"),
        name="mla_down",
    )(*args)


def _sink_softmax_pv(s, sk, v):
    mx = jnp.maximum(jnp.max(s, axis=1, keepdims=True), sk)
    e = jnp.exp(s - mx)
    r = 1.0 / (jnp.sum(e, axis=1, keepdims=True) + jnp.exp(sk - mx))
    return jnp.dot((e * r).astype(BF16), v, preferred_element_type=F32)


def _win_attn_kernel(sink_ref, q_ref, kp_ref, kc_ref, kn_ref, kx_ref, vp_ref, vc_ref, vn_ref, vx_ref, o_ref,
                     *, tq, n, R):
    g = pl.program_id(1)
    qi = pl.program_id(2)
    k = jnp.concatenate([kp_ref[0], kc_ref[0], kn_ref[0], kx_ref[0]], axis=0)
    v = jnp.concatenate([vp_ref[0], vc_ref[0], vn_ref[0], vx_ref[0]], axis=0)
    L = k.shape[0]
    q_pos = qi * tq + lax.broadcasted_iota(jnp.int32, (tq, L), 0)
    col = lax.broadcasted_iota(jnp.int32, (tq, L), 1)
    key_pos = (qi - 1) * tq + col
    valid = ((jnp.abs(q_pos - key_pos) <= WINDOW) & (key_pos >= 0) & (key_pos < n)) | (col >= 3 * tq)
    q = q_ref[0]
    scale = HEAD ** -0.5
    for r in range(R):
        sl = slice(r * HEAD, (r + 1) * HEAD)
        s = jnp.where(valid, _dot_nt(q[:, sl], k) * scale, NEG_INF)
        o_ref[0, :, sl] = _sink_softmax_pv(s, sink_ref[g * R + r], v).astype(o_ref.dtype)


def _win_attn(qkv, qkv_ctx, sink, H, G):
    B, n, _ = qkv.shape
    m = qkv_ctx.shape[1]
    R = H // G
    tq = _tile(n, WINDOW)
    assert tq == WINDOW, "window kernel walks previous/current/next key blocks of WINDOW rows"
    nb = n // tq
    prev = lambda i: jnp.maximum(i - 1, 0)
    nxt = lambda i: jnp.minimum(i + 1, nb - 1)
    kspec = lambda f, off: pl.BlockSpec((1, tq, HEAD), lambda b, g, i: (b, f(i), off + g))
    xspec = lambda off: pl.BlockSpec((1, m, HEAD), lambda b, g, i: (b, 0, off + g))
    ident = lambda i: i
    return pl.pallas_call(
        functools.partial(_win_attn_kernel, tq=tq, n=n, R=R),
        grid=(B, G, nb),
        in_specs=[pl.BlockSpec(memory_space=pltpu.SMEM),
                  pl.BlockSpec((1, tq, R * HEAD), lambda b, g, i: (b, i, g)),
                  kspec(prev, H), kspec(ident, H), kspec(nxt, H), xspec(H),
                  kspec(prev, H + G), kspec(ident, H + G), kspec(nxt, H + G), xspec(H + G)],
        out_specs=pl.BlockSpec((1, tq, R * HEAD), lambda b, g, i: (b, i, g)),
        out_shape=jax.ShapeDtypeStruct((B, n, H * HEAD), BF16),
        compiler_params=_params("parallel", "parallel", "parallel"),
        name="win_attn",
    )(sink.astype(F32), qkv, qkv, qkv, qkv, qkv_ctx, qkv, qkv, qkv, qkv_ctx)


def _sink_attn_kernel(sink_ref, q_ref, k_ref, v_ref, o_ref, *, R):
    g = pl.program_id(1)
    q, k, v = q_ref[0], k_ref[0], v_ref[0]
    scale = HEAD ** -0.5
    for r in range(R):
        sl = slice(r * HEAD, (r + 1) * HEAD)
        s = _dot_nt(q[:, sl], k) * scale
        o_ref[0, :, sl] = _sink_softmax_pv(s, sink_ref[g * R + r], v).astype(o_ref.dtype)


def _sink_attn(qkv, sink, H, G):
    B, m, _ = qkv.shape
    R = H // G
    return pl.pallas_call(
        functools.partial(_sink_attn_kernel, R=R),
        grid=(B, G),
        in_specs=[pl.BlockSpec(memory_space=pltpu.SMEM),
                  pl.BlockSpec((1, m, R * HEAD), lambda b, g: (b, 0, g)),
                  pl.BlockSpec((1, m, HEAD), lambda b, g: (b, 0, H + g)),
                  pl.BlockSpec((1, m, HEAD), lambda b, g: (b, 0, H + G + g))],
        out_specs=pl.BlockSpec((1, m, R * HEAD), lambda b, g: (b, 0, g)),
        out_shape=jax.ShapeDtypeStruct((B, m, H * HEAD), BF16),
        compiler_params=_params("parallel", "parallel"),
        name="sink_attn",
    )(sink.astype(F32), qkv, qkv, qkv)


def _silu(g):
    return g * jax.nn.sigmoid(g)


def _ffn_kernel(x_ref, sh_ref, sc_ref, gate_ref, w1_ref, w3_ref, w2_ref, g_ref, b_ref, o_ref, h_ref, acc_ref,
                *, alpha, nf):
    f = pl.program_id(2)

    @pl.when(f == 0)
    def _():
        h_ref[...] = (x_ref[0] * (1.0 + sc_ref[0]) + sh_ref[0]).astype(BF16)
        acc_ref[...] = jnp.zeros(acc_ref.shape, F32)

    h = h_ref[...]
    a = _silu(jnp.dot(h, w1_ref[...], preferred_element_type=F32)) * jnp.dot(h, w3_ref[...], preferred_element_type=F32)
    acc_ref[...] += jnp.dot(a.astype(BF16), w2_ref[...], preferred_element_type=F32)

    @pl.when(f == nf - 1)
    def _():
        o_ref[0] = _res_ln(x_ref[0], acc_ref[...], gate_ref[0], g_ref[...], b_ref[...], alpha)


def _ffn(x, w13, w2, mod, g, b, *, ctx, alpha):
    B, R, D = x.shape
    Fd = w2.shape[0]
    tm = _tile(R, TM)
    tf = _tile(Fd, TF, LANE)
    nf = Fd // tf
    row = _mod_row(ctx, B)
    vec = pl.BlockSpec((1, 1, D), lambda b, i, f: (row(b), 0, 0))
    return pl.pallas_call(
        functools.partial(_ffn_kernel, alpha=alpha, nf=nf),
        grid=(B, R // tm, nf),
        in_specs=[pl.BlockSpec((1, tm, D), lambda b, i, f: (b, i, 0)), vec, vec, vec,
                  pl.BlockSpec((D, tf), lambda b, i, f: (0, f)),
                  pl.BlockSpec((D, tf), lambda b, i, f: (0, nf + f)),
                  pl.BlockSpec((tf, D), lambda b, i, f: (f, 0)),
                  pl.BlockSpec((1, D), lambda b, i, f: (0, 0)),
                  pl.BlockSpec((1, D), lambda b, i, f: (0, 0))],
        out_specs=pl.BlockSpec((1, tm, D), lambda b, i, f: (b, i, 0)),
        out_shape=jax.ShapeDtypeStruct((B, R, D), F32),
        scratch_shapes=[pltpu.VMEM((tm, D), BF16), pltpu.VMEM((tm, D), F32)],
        compiler_params=_params("parallel", "parallel", "arbitrary"),
        name="ffn",
    )(x, mod[0], mod[1], mod[2], w13, w13, w2, g.reshape(1, D), b.reshape(1, D))


def _router_kernel(x_ref, sh_ref, sc_ref, rt_ref, h_ref, idx_ref, wts_ref):
    hf = x_ref[0] * (1.0 + sc_ref[0]) + sh_ref[0]
    h_ref[0] = hf
    logits = jnp.dot(hf, rt_ref[...], preferred_element_type=F32, precision=lax.Precision.HIGHEST)
    E = logits.shape[1]
    lane = lax.broadcasted_iota(jnp.int32, logits.shape, 1)
    m1 = jnp.max(logits, axis=1, keepdims=True)
    i1 = jnp.min(jnp.where(logits == m1, lane, E), axis=1, keepdims=True)
    rest = jnp.where(lane == i1, -jnp.inf, logits)
    m2 = jnp.max(rest, axis=1, keepdims=True)
    i2 = jnp.min(jnp.where(rest == m2, lane, E), axis=1, keepdims=True)
    w1 = 1.0 / (1.0 + jnp.exp(m2 - m1))
    two = lax.broadcasted_iota(jnp.int32, idx_ref.shape[1:], 1)
    idx_ref[0] = jnp.where(two == 0, i1, i2)
    wts_ref[0] = jnp.where(two == 0, w1, 1.0 - w1)


def _router(x, router, mod, *, ctx):
    B, R, D = x.shape
    E = router.shape[1]
    tm = _tile(R, TM)
    row = _mod_row(ctx, B)
    vec = pl.BlockSpec((1, 1, D), lambda b, i: (row(b), 0, 0))
    return pl.pallas_call(
        _router_kernel,
        grid=(B, R // tm),
        in_specs=[pl.BlockSpec((1, tm, D), lambda b, i: (b, i, 0)), vec, vec,
                  pl.BlockSpec((D, E), lambda b, i: (0, 0))],
        out_specs=[pl.BlockSpec((1, tm, D), lambda b, i: (b, i, 0)),
                   pl.BlockSpec((1, tm, 2), lambda b, i: (b, i, 0)),
                   pl.BlockSpec((1, tm, 2), lambda b, i: (b, i, 0))],
        out_shape=[jax.ShapeDtypeStruct((B, R, D), F32),
                   jax.ShapeDtypeStruct((B, R, 2), jnp.int32),
                   jax.ShapeDtypeStruct((B, R, 2), F32)],
        compiler_params=_params("parallel", "parallel"),
        name="router",
    )(x, mod[0], mod[1], router)


def _route_plan(idx, wts, E, tm):
    T = idx.shape[0]
    P = 2 * T
    e_flat = idx.reshape(P)
    onehot = (e_flat[:, None] == jnp.arange(E, dtype=jnp.int32)[None, :]).astype(jnp.int32)
    counts = jnp.sum(onehot, axis=0)
    rank = jnp.take_along_axis(jnp.cumsum(onehot, axis=0) - onehot, e_flat[:, None], axis=1)[:, 0]
    padded = ((counts + tm - 1) // tm) * tm
    ends = jnp.cumsum(padded)
    starts = ends - padded
    slot = starts[e_flat] + rank
    ntiles = P // tm + E
    S = ntiles * tm
    tile_start = jnp.arange(ntiles, dtype=jnp.int32) * tm
    tile_expert = jnp.minimum(jnp.searchsorted(ends, tile_start, side="right"), E - 1).astype(jnp.int32)
    tile_rows = jnp.clip(counts[tile_expert] - (tile_start - starts[tile_expert]), 0, tm)
    tile_rows = jnp.where(tile_start < ends[E - 1], tile_rows, 0).astype(jnp.int32)
    last = jnp.maximum(jnp.sum((tile_rows > 0).astype(jnp.int32)) - 1, 0)
    tile_expert = jnp.where(tile_rows > 0, tile_expert, tile_expert[last])
    pair = jnp.arange(P, dtype=jnp.int32)
    token, choice = pair // 2, pair % 2
    src = jnp.zeros((S,), jnp.int32).at[slot].set(token)
    dst = jnp.zeros((S,), jnp.int32).at[slot].set(choice * T + token)
    gate = jnp.zeros((S,), F32).at[slot].set(wts.reshape(P))
    return tile_expert, tile_rows, src.reshape(ntiles, 1, tm), dst.reshape(ntiles, 1, tm), gate.reshape(S, 1)


def _experts_kernel(te_ref, tr_ref, src_ref, dst_ref, gate_ref, h_hbm, w1_ref, w3_ref, w2_ref, y_hbm,
                    rows_ref, hb_ref, acc_ref, sem_in, sem_out, *, nf, tm):
    t = pl.program_id(0)
    f = pl.program_id(1)
    nrows = tr_ref[t]
    valid = nrows > 0

    def row_dmas(copy_row):
        ngroups = nrows // ROW_UNROLL

        def group(i, carry):
            for u in range(ROW_UNROLL):
                copy_row(i * ROW_UNROLL + u).start()
            return carry

        def single(r, carry):
            copy_row(r).start()
            return carry

        lax.fori_loop(0, ngroups, group, 0)
        lax.fori_loop(ngroups * ROW_UNROLL, nrows, single, 0)

    def wait_rows(copy_block, copy_row):
        nfull = pl.multiple_of((nrows // SUBLANE) * SUBLANE, SUBLANE)

        @pl.when(nfull > 0)
        def _():
            copy_block(nfull).wait()

        def single(r, carry):
            copy_row(r).wait()
            return carry

        lax.fori_loop(nfull, nrows, single, 0)

    @pl.when((t == 0) & (f == 0))
    def _():
        rows_ref[...] = jnp.zeros(rows_ref.shape, F32)

    @pl.when(valid & (f == 0))
    def _():
        gather_row = lambda r: pltpu.make_async_copy(h_hbm.at[pl.ds(src_ref[0, 0, r], 1)],
                                                     rows_ref.at[pl.ds(r, 1)], sem_in)
        row_dmas(gather_row)
        wait_rows(lambda n: pltpu.make_async_copy(h_hbm.at[pl.ds(0, n)], rows_ref.at[pl.ds(0, n)], sem_in),
                  gather_row)
        hb_ref[...] = rows_ref[...].astype(BF16)
        acc_ref[...] = jnp.zeros(acc_ref.shape, F32)

    @pl.when(valid)
    def _():
        h = hb_ref[...]
        a = _silu(jnp.dot(h, w1_ref[0], preferred_element_type=F32)) * jnp.dot(h, w3_ref[0], preferred_element_type=F32)
        acc_ref[...] += jnp.dot(a.astype(BF16), w2_ref[0], preferred_element_type=F32)

    @pl.when(valid & (f == nf - 1))
    def _():
        rows_ref[...] = acc_ref[...] * gate_ref[...]
        scatter_row = lambda r: pltpu.make_async_copy(rows_ref.at[pl.ds(r, 1)],
                                                      y_hbm.at[pl.ds(dst_ref[0, 0, r], 1)], sem_out)
        row_dmas(scatter_row)
        wait_rows(lambda n: pltpu.make_async_copy(rows_ref.at[pl.ds(0, n)], y_hbm.at[pl.ds(0, n)], sem_out),
                  scatter_row)


def _experts(h, plan, w13, w2, tm):
    T, D = h.shape
    E, Fd, _ = w2.shape
    tile_expert, tile_rows, src, dst, gate = plan
    ntiles = src.shape[0]
    tf = _tile(Fd, TF, LANE)
    nf = Fd // tf
    fidx = lambda t, f, tv: jnp.where(tv[t] > 0, f, nf - 1)
    grid_spec = pltpu.PrefetchScalarGridSpec(
        num_scalar_prefetch=2,
        grid=(ntiles, nf),
        in_specs=[pl.BlockSpec((1, 1, tm), lambda t, f, te, tv: (t, 0, 0), memory_space=pltpu.SMEM),
                  pl.BlockSpec((1, 1, tm), lambda t, f, te, tv: (t, 0, 0), memory_space=pltpu.SMEM),
                  pl.BlockSpec((tm, 1), lambda t, f, te, tv: (t, 0)),
                  pl.BlockSpec(memory_space=pl.ANY),
                  pl.BlockSpec((1, D, tf), lambda t, f, te, tv: (te[t], 0, fidx(t, f, tv))),
                  pl.BlockSpec((1, D, tf), lambda t, f, te, tv: (te[t], 0, nf + fidx(t, f, tv))),
                  pl.BlockSpec((1, tf, D), lambda t, f, te, tv: (te[t], fidx(t, f, tv), 0))],
        out_specs=pl.BlockSpec(memory_space=pl.ANY),
        scratch_shapes=[pltpu.VMEM((tm, D), F32), pltpu.VMEM((tm, D), BF16), pltpu.VMEM((tm, D), F32),
                        pltpu.SemaphoreType.DMA(()), pltpu.SemaphoreType.DMA(())],
    )
    return pl.pallas_call(
        functools.partial(_experts_kernel, nf=nf, tm=tm),
        grid_spec=grid_spec,
        out_shape=jax.ShapeDtypeStruct((2 * T, D), F32),
        compiler_params=pltpu.CompilerParams(dimension_semantics=("arbitrary", "arbitrary"),
                                             vmem_limit_bytes=VMEM_LIMIT),
        name="experts",
    )(tile_expert, tile_rows, src, dst, gate, h, w13, w13, w2)


def _combine_kernel(x_ref, y0_ref, y1_ref, gate_ref, g_ref, b_ref, o_ref, *, alpha):
    o_ref[0] = _res_ln(x_ref[0], y0_ref[...] + y1_ref[...], gate_ref[0], g_ref[...], b_ref[...], alpha)


def _combine_ln(x, y, gate, g, b, *, ctx, alpha):
    B, R, D = x.shape
    tm = _tile(R, TM)
    nb = R // tm
    row = _mod_row(ctx, B)
    return pl.pallas_call(
        functools.partial(_combine_kernel, alpha=alpha),
        grid=(B, nb),
        in_specs=[pl.BlockSpec((1, tm, D), lambda b, i: (b, i, 0)),
                  pl.BlockSpec((tm, D), lambda b, i: (b * nb + i, 0)),
                  pl.BlockSpec((tm, D), lambda b, i: (B * nb + b * nb + i, 0)),
                  pl.BlockSpec((1, 1, D), lambda b, i: (row(b), 0, 0)),
                  pl.BlockSpec((1, D), lambda b, i: (0, 0)),
                  pl.BlockSpec((1, D), lambda b, i: (0, 0))],
        out_specs=pl.BlockSpec((1, tm, D), lambda b, i: (b, i, 0)),
        out_shape=jax.ShapeDtypeStruct((B, R, D), F32),
        compiler_params=_params("parallel", "parallel"),
        name="combine_ln",
    )(x, y, y, gate, g.reshape(1, D), b.reshape(1, D))


def _moe(x, router, w13, w2, mod, g, b, *, ctx, alpha):
    B, R, D = x.shape
    E = router.shape[1]
    T = B * R
    tm = _tile(2 * T, TME)
    h, idx, wts = _router(x, router, mod, ctx=ctx)
    plan = _route_plan(idx.reshape(T, 2), wts.reshape(T, 2), E, tm)
    y = _experts(h.reshape(T, D), plan, w13, w2, tm)
    return _combine_ln(x, y, mod[2], g, b, ctx=ctx, alpha=alpha)


def _mix_diff(xl, xc, mod, need_ctx, layer, wqkv, lam, subln_g, rope128):
    D = xl.shape[2]
    H = D // (2 * HEAD)
    lam_init = 0.8 - 0.6 * math.exp(-0.3 * layer)
    w = wqkv.astype(BF16)
    qkv_l = _proj(xl, w, mod=mod, ctx=False, rope=rope128, rope_cols=(0, 2 * D))
    qkv_c = _proj(xc, w, mod=mod, ctx=True)
    kv_all = jnp.concatenate([qkv_l, qkv_c], axis=1)
    o_l = _diff_attn(qkv_l, kv_all, lam, subln_g, lam_init, H)
    o_c = _diff_attn(qkv_c, qkv_c, lam, subln_g, lam_init, H) if need_ctx else None
    return o_l, o_c


def _mix_window(xl, xc, mod, need_ctx, wqkv, sink, rope128):
    D = xl.shape[2]
    H = D // HEAD
    G = H // 4
    w = wqkv.astype(BF16)
    qkv_l = _proj(xl, w, mod=mod, ctx=False, rope=rope128, rope_cols=(0, (H + G) * HEAD))
    qkv_c = _proj(xc, w, mod=mod, ctx=True)
    o_l = _win_attn(qkv_l, qkv_c, sink, H, G)
    o_c = _sink_attn(qkv_c, sink, H, G) if need_ctx else None
    return o_l, o_c


def _mix_mla(xl, xc, mod, need_ctx, w_down, gq, gkv, w_uq, w_ukv, rope64):
    D = xl.shape[2]
    H = D // HEAD
    rq, rkv = gq.shape[0], gkv.shape[0]
    wd = jnp.pad(w_down, ((0, 0), (0, LANE - MLA_ROPE))).astype(BF16)
    wq3 = w_uq.reshape(rq, H, HEAD + MLA_ROPE)
    wq = jnp.concatenate(
        [wq3[:, :, :HEAD].reshape(rq, H * HEAD),
         jnp.pad(wq3[:, :, HEAD:], ((0, 0), (0, 0), (0, HEAD - MLA_ROPE))).reshape(rq, H * HEAD)], axis=1).astype(BF16)
    wkv3 = w_ukv.reshape(rkv, H, 2 * HEAD)
    wkv = jnp.concatenate([wkv3[:, :, :HEAD].reshape(rkv, H * HEAD),
                           wkv3[:, :, HEAD:].reshape(rkv, H * HEAD)], axis=1).astype(BF16)
    cq_l, ckv_l, kr_l = _mla_down(xl, wd, gq, gkv, mod, ctx=False, rope=rope64)
    cq_c, ckv_c, kr_c = _mla_down(xc, wd, gq, gkv, mod, ctx=True, rope=None)
    q_l = _proj(cq_l, wq, rope=rope64, rope_cols=(H * HEAD, 2 * H * HEAD))
    kv_l = _proj(ckv_l, wkv)
    kv_c = _proj(ckv_c, wkv)
    kv_all = jnp.concatenate([kv_l, kv_c], axis=1)
    kr_all = jnp.concatenate([kr_l, kr_c], axis=1)
    o_l = _mla_attn(q_l, kv_all, kr_all, H)
    o_c = None
    if need_ctx:
        q_c = _proj(cq_c, wq)
        o_c = _mla_attn(q_c, kv_c, kr_c, H)
    return o_l, o_c


def kernel(x, c, ctx, c_ctx, mod_w, mod_b, ln1_g, ln1_b, ln2_g, ln2_b, da_wqkv, da_lambda, da_subln_g, da_wo, wa_wqkv, wa_sink, wa_wo, mla_w_down, mla_q_norm_g, mla_kv_norm_g, mla_w_uq, mla_w_ukv, mla_wo, ffn_w13, ffn_w2, moe_router, moe_w13, moe_w2):
    B, n, D = x.shape
    depth = mod_w.shape[0]
    alpha = (2.0 * depth) ** 0.25
    rows = SUBLANE * (-(-(B + 1) // SUBLANE))
    cc = jnp.concatenate([c, c_ctx[None, :], jnp.zeros((rows - B - 1, D), F32)], axis=0)
    mod_all = _modulation(cc, mod_w, mod_b)
    rope128 = _rope_tables(n, HEAD)
    rope64 = _rope_tables(n, MLA_ROPE)
    xl, xc = x, ctx
    for i in range(depth):
        need_ctx = i < depth - 1
        mod = [mod_all[i, :, k * D:(k + 1) * D].reshape(rows, 1, D) for k in range(6)]
        kind, slot = i % N_MIXERS, i // N_MIXERS
        if kind == 0:
            o_l, o_c = _mix_diff(xl, xc, mod[0:2], need_ctx, i, da_wqkv[slot], da_lambda[slot],
                                 da_subln_g[slot], rope128)
            wo = da_wo[slot]
        elif kind == 1:
            o_l, o_c = _mix_window(xl, xc, mod[0:2], need_ctx, wa_wqkv[slot], wa_sink[slot], rope128)
            wo = wa_wo[slot]
        else:
            o_l, o_c = _mix_mla(xl, xc, mod[0:2], need_ctx, mla_w_down[slot], mla_q_norm_g[slot],
                                mla_kv_norm_g[slot], mla_w_uq[slot], mla_w_ukv[slot], rope64)
            wo = mla_wo[slot]
        wo = wo.astype(BF16)
        xl = _oproj_ln(xl, o_l, wo, mod[2], ln1_g[i], ln1_b[i], ctx=False, alpha=alpha)
        if need_ctx:
            xc = _oproj_ln(xc, o_c, wo, mod[2], ln1_g[i], ln1_b[i], ctx=True, alpha=alpha)
        cslot = i // 2
        if i % 2 == 0:
            w13, w2 = ffn_w13[cslot].astype(BF16), ffn_w2[cslot].astype(BF16)
            xl = _ffn(xl, w13, w2, mod[3:6], ln2_g[i], ln2_b[i], ctx=False, alpha=alpha)
            if need_ctx:
                xc = _ffn(xc, w13, w2, mod[3:6], ln2_g[i], ln2_b[i], ctx=True, alpha=alpha)
        else:
            w13, w2 = moe_w13[cslot].astype(BF16), moe_w2[cslot].astype(BF16)
            xl = _moe(xl, moe_router[cslot], w13, w2, mod[3:6], ln2_g[i], ln2_b[i], ctx=False, alpha=alpha)
            if need_ctx:
                xc = _moe(xc, moe_router[cslot], w13, w2, mod[3:6], ln2_g[i], ln2_b[i], ctx=True, alpha=alpha)
    return xl
```

```python
import functools
import math

import jax
import jax.numpy as jnp
from jax import lax
from jax.experimental import pallas as pl
from jax.experimental.pallas import tpu as pltpu

F32 = jnp.float32
BF16 = jnp.bfloat16

LANE = 128
SUBLANE = 8
VMEM_LIMIT = 56 * 1024 * 1024

GRID_W = 64
ROPE_BASE = 10000.0
LN_EPS = 1e-5
RMS_EPS = 1e-6
NEG_INF = -1e30
HEAD = 128
WINDOW = 128
MLA_ROPE = 64
N_MIXERS = 3
LOG2E = math.log2(math.e)

TM = 512
TMP = 1024
TW = 512
TN = 1024
TF = 512
TQ = 1024
TK = 2816
RB = 256
TME = 1024
ROW_UNROLL = 8


def _tile(dim, pref, align=SUBLANE):
    if dim <= pref:
        return dim
    t = (pref // align) * align
    while t >= align:
        if dim % t == 0:
            return t
        t -= align
    return dim


def _params(*sem):
    return pltpu.CompilerParams(dimension_semantics=sem, vmem_limit_bytes=VMEM_LIMIT)


def _mod_row(ctx, nb):
    return (lambda b: nb) if ctx else (lambda b: b)


def _mod_kernel(s_ref, w_ref, b_ref, o_ref):
    s = s_ref[...]
    s = s * jax.nn.sigmoid(s)
    o_ref[0] = jnp.dot(s.astype(BF16), w_ref[0].astype(BF16), preferred_element_type=F32) + b_ref[0]


def _modulation(cc, mod_w, mod_b):
    depth, D, N = mod_w.shape
    R = cc.shape[0]
    tn = _tile(N, 1024, LANE)
    return pl.pallas_call(
        _mod_kernel,
        grid=(depth, N // tn),
        in_specs=[pl.BlockSpec((R, D), lambda l, j: (0, 0)),
                  pl.BlockSpec((1, D, tn), lambda l, j: (l, 0, j)),
                  pl.BlockSpec((1, 1, tn), lambda l, j: (l, 0, j))],
        out_specs=pl.BlockSpec((1, R, tn), lambda l, j: (l, 0, j)),
        out_shape=jax.ShapeDtypeStruct((depth, R, N), F32),
        compiler_params=_params("parallel", "parallel"),
        name="modulation",
    )(cc, mod_w, mod_b.reshape(depth, 1, N))


def _rope_tables(n, dim):
    half = dim // 4
    t = jnp.arange(n)
    row = (t // GRID_W).astype(F32)
    col = (t % GRID_W).astype(F32)
    inv = ROPE_BASE ** (-jnp.arange(half, dtype=F32) / half)
    ar = row[:, None] * inv[None, :]
    ac = col[:, None] * inv[None, :]
    cos = jnp.concatenate([jnp.cos(ar), jnp.cos(ar), jnp.cos(ac), jnp.cos(ac)], axis=1)
    sin = jnp.concatenate([-jnp.sin(ar), jnp.sin(ar), -jnp.sin(ac), jnp.sin(ac)], axis=1)
    rep = LANE // dim
    return jnp.tile(cos, (1, rep)), jnp.tile(sin, (1, rep)), half


def _rope_chunk(yc, cos, sin, half):
    lane = lax.broadcasted_iota(jnp.int32, (1, LANE), 1)
    first = (lane % (2 * half)) < half
    rot = jnp.where(first, pltpu.roll(yc, LANE - half, 1), pltpu.roll(yc, half, 1))
    return yc * cos + rot * sin


def _proj_kernel(*refs, modulate, rope_lo, rope_hi, half, q_hi, q_scale):
    refs = list(refs)
    x_ref = refs.pop(0)
    if modulate:
        sh_ref, sc_ref = refs.pop(0), refs.pop(0)
    w_ref = refs.pop(0)
    if rope_hi > rope_lo:
        cos_ref, sin_ref = refs.pop(0), refs.pop(0)
    o_ref = refs.pop(0)
    j = pl.program_id(2)
    if modulate:
        h_ref = refs.pop(0)

        @pl.when(j == 0)
        def _():
            h_ref[...] = (x_ref[0] * (1.0 + sc_ref[0]) + sh_ref[0]).astype(BF16)

        h = h_ref[...]
    else:
        h = x_ref[0]
    y = jnp.dot(h, w_ref[...], preferred_element_type=F32)
    if q_hi > 0:
        y = y * jnp.where(j < q_hi, q_scale, 1.0)
    if rope_hi > rope_lo:
        in_rope = (j >= rope_lo) & (j < rope_hi)

        @pl.when(in_rope)
        def _():
            cos, sin = cos_ref[...], sin_ref[...]
            for cidx in range(y.shape[1] // LANE):
                sl = slice(cidx * LANE, (cidx + 1) * LANE)
                o_ref[0, :, sl] = _rope_chunk(y[:, sl], cos, sin, half).astype(o_ref.dtype)

        @pl.when(jnp.logical_not(in_rope))
        def _():
            o_ref[0] = y.astype(o_ref.dtype)
    else:
        o_ref[0] = y.astype(o_ref.dtype)


def _proj(x, w, *, mod=None, ctx=False, rope=None, rope_cols=(0, 0), q_cols=0, q_scale=1.0):
    B, R, K = x.shape
    N = w.shape[1]
    tm = _tile(R, TMP)
    g = math.gcd(N, q_cols)
    if rope is not None:
        g = math.gcd(g, math.gcd(rope_cols[0], rope_cols[1]))
    tn = _tile(g, TN, LANE)
    modulate = mod is not None
    args = [x]
    in_specs = [pl.BlockSpec((1, tm, K), lambda b, i, j: (b, i, 0))]
    if modulate:
        shift, scale = mod
        row = _mod_row(ctx, B)
        in_specs += [pl.BlockSpec((1, 1, K), lambda b, i, j: (row(b), 0, 0))] * 2
        args += [shift, scale]
    in_specs.append(pl.BlockSpec((K, tn), lambda b, i, j: (0, j)))
    args.append(w)
    half = 0
    if rope is not None:
        cos, sin, half = rope
        in_specs += [pl.BlockSpec((tm, LANE), lambda b, i, j: (i, 0))] * 2
        args += [cos, sin]
    kern = functools.partial(_proj_kernel, modulate=modulate, rope_lo=rope_cols[0] // tn,
                             rope_hi=rope_cols[1] // tn if rope is not None else 0, half=half,
                             q_hi=q_cols // tn, q_scale=q_scale)
    return pl.pallas_call(
        kern,
        grid=(B, R // tm, N // tn),
        in_specs=in_specs,
        out_specs=pl.BlockSpec((1, tm, tn), lambda b, i, j: (b, i, j)),
        out_shape=jax.ShapeDtypeStruct((B, R, N), BF16),
        scratch_shapes=[pltpu.VMEM((tm, K), BF16)] if modulate else [],
        compiler_params=_params("parallel", "parallel", "arbitrary"),
        name="proj",
    )(*args)


def _res_ln(x, y, gate, g, b, alpha):
    z = alpha * x + gate * y
    mu = jnp.mean(z, axis=-1, keepdims=True)
    zc = z - mu
    var = jnp.mean(zc * zc, axis=-1, keepdims=True)
    return zc * lax.rsqrt(var + LN_EPS) * g + b


def _oproj_kernel(x_ref, o_ref, w_ref, gate_ref, g_ref, b_ref, out_ref, *, alpha):
    y = jnp.dot(o_ref[0], w_ref[...], preferred_element_type=F32)
    out_ref[0] = _res_ln(x_ref[0], y, gate_ref[0], g_ref[...], b_ref[...], alpha)


def _oproj_ln(x, o, w, gate, g, b, *, ctx, alpha):
    B, R, D = x.shape
    K = o.shape[2]
    tm = _tile(R, TM)
    row = _mod_row(ctx, B)
    return pl.pallas_call(
        functools.partial(_oproj_kernel, alpha=alpha),
        grid=(B, R // tm),
        in_specs=[pl.BlockSpec((1, tm, D), lambda b, i: (b, i, 0)),
                  pl.BlockSpec((1, tm, K), lambda b, i: (b, i, 0)),
                  pl.BlockSpec((K, D), lambda b, i: (0, 0)),
                  pl.BlockSpec((1, 1, D), lambda b, i: (row(b), 0, 0)),
                  pl.BlockSpec((1, D), lambda b, i: (0, 0)),
                  pl.BlockSpec((1, D), lambda b, i: (0, 0))],
        out_specs=pl.BlockSpec((1, tm, D), lambda b, i: (b, i, 0)),
        out_shape=jax.ShapeDtypeStruct((B, R, D), F32),
        compiler_params=_params("parallel", "parallel"),
        name="oproj_ln",
    )(x, o, w, gate, g.reshape(1, D), b.reshape(1, D))


def _flash_init(m_ref, l_ref, acc_ref):
    m_ref[...] = jnp.full(m_ref.shape, NEG_INF, F32)
    l_ref[...] = jnp.zeros(l_ref.shape, F32)
    acc_ref[...] = jnp.zeros(acc_ref.shape, F32)


def _dot_nt(a, b):
    return lax.dot_general(a, b, (((1,), (1,)), ((), ())), preferred_element_type=F32)


def _flash_step(q_fn, k_fn, v_fn, m_ref, l_ref, acc_ref, *, chains, tq):
    rb = min(RB, tq)

    def body(r, carry):
        rows = pl.ds(pl.multiple_of(r * rb, rb), rb)
        for c in range(chains):
            s = _dot_nt(q_fn(c, rows), k_fn(c))
            m_prev = m_ref[c, rows]
            m_new = jnp.maximum(m_prev, jnp.max(s, axis=1, keepdims=True))
            a = jnp.exp2(m_prev - m_new)
            p = jnp.exp2(s - m_new)
            l_ref[c, rows] = a * l_ref[c, rows] + jnp.sum(p, axis=1, keepdims=True)
            acc_ref[c, rows] = a * acc_ref[c, rows] + jnp.dot(p.astype(BF16), v_fn(c), preferred_element_type=F32)
            m_ref[c, rows] = m_new
        return carry

    lax.fori_loop(0, tq // rb, body, 0, unroll=4)


def _diff_attn_kernel(lam_ref, g_ref, q_ref, k_ref, v_ref, o_ref, m_ref, l_ref, acc_ref, *, lam_init, nk):
    ki = pl.program_id(3)

    @pl.when(ki == 0)
    def _():
        _flash_init(m_ref, l_ref, acc_ref)

    _flash_step(lambda c, rows: q_ref[0, rows, c * HEAD:(c + 1) * HEAD],
                lambda c: k_ref[0, :, c * HEAD:(c + 1) * HEAD],
                lambda c: v_ref[0],
                m_ref, l_ref, acc_ref, chains=2, tq=q_ref.shape[1])

    @pl.when(ki == nk - 1)
    def _():
        lam = lam_ref[...]
        lam_full = (jnp.exp(jnp.sum(lam[0:1] * lam[1:2], axis=1, keepdims=True))
                    - jnp.exp(jnp.sum(lam[2:3] * lam[3:4], axis=1, keepdims=True)) + lam_init)
        o = acc_ref[0] / l_ref[0] - lam_full * (acc_ref[1] / l_ref[1])
        ms = jnp.mean(o * o, axis=1, keepdims=True)
        o_ref[0] = (o * lax.rsqrt(ms + RMS_EPS) * g_ref[...] * (1.0 - lam_init)).astype(o_ref.dtype)


def _diff_attn(q_src, kv_src, lam, subln_g, lam_init, H):
    B, nq, _ = q_src.shape
    nkv = kv_src.shape[1]
    W = 2 * HEAD
    tq = _tile(nq, TQ)
    tk = _tile(nkv, TK)
    nk = nkv // tk
    return pl.pallas_call(
        functools.partial(_diff_attn_kernel, lam_init=lam_init, nk=nk),
        grid=(B, H, nq // tq, nk),
        in_specs=[pl.BlockSpec((4, HEAD), lambda b, h, i, k: (0, 0)),
                  pl.BlockSpec((1, W), lambda b, h, i, k: (0, 0)),
                  pl.BlockSpec((1, tq, W), lambda b, h, i, k: (b, i, h)),
                  pl.BlockSpec((1, tk, W), lambda b, h, i, k: (b, k, H + h)),
                  pl.BlockSpec((1, tk, W), lambda b, h, i, k: (b, k, 2 * H + h))],
        out_specs=pl.BlockSpec((1, tq, W), lambda b, h, i, k: (b, i, h)),
        out_shape=jax.ShapeDtypeStruct((B, nq, H * W), BF16),
        scratch_shapes=[pltpu.VMEM((2, tq, 1), F32), pltpu.VMEM((2, tq, 1), F32),
                        pltpu.VMEM((2, tq, W), F32)],
        compiler_params=_params("parallel", "parallel", "parallel", "arbitrary"),
        name="diff_attn",
    )(lam.astype(F32), subln_g.reshape(1, W), q_src, kv_src, kv_src)


def _mla_attn_kernel(qn_ref, qr_ref, kn_ref, kr_ref, v_ref, o_ref, m_ref, l_ref, acc_ref, k_ref, *, nk):
    ki = pl.program_id(3)

    @pl.when(ki == 0)
    def _():
        _flash_init(m_ref, l_ref, acc_ref)

    for c in range(2):
        k_ref[c, :, :HEAD] = kn_ref[0, :, c * HEAD:(c + 1) * HEAD]
        k_ref[c, :, HEAD:] = kr_ref[0]

    def q_fn(c, rows):
        sl = slice(c * HEAD, (c + 1) * HEAD)
        return jnp.concatenate([qn_ref[0, rows, sl], qr_ref[0, rows, sl]], axis=1)

    _flash_step(q_fn, lambda c: k_ref[c], lambda c: v_ref[0, :, c * HEAD:(c + 1) * HEAD],
                m_ref, l_ref, acc_ref, chains=2, tq=qn_ref.shape[1])

    @pl.when(ki == nk - 1)
    def _():
        for c in range(2):
            o_ref[0, :, c * HEAD:(c + 1) * HEAD] = (acc_ref[c] / l_ref[c]).astype(o_ref.dtype)


def _mla_attn(q_src, kv_src, kr_src, H):
    B, nq, _ = q_src.shape
    nkv = kv_src.shape[1]
    assert H % 2 == 0
    P = H // 2
    W = 2 * HEAD
    tq = _tile(nq, TQ)
    tk = _tile(nkv, TK)
    nk = nkv // tk
    return pl.pallas_call(
        functools.partial(_mla_attn_kernel, nk=nk),
        grid=(B, P, nq // tq, nk),
        in_specs=[pl.BlockSpec((1, tq, W), lambda b, h, i, k: (b, i, h)),
                  pl.BlockSpec((1, tq, W), lambda b, h, i, k: (b, i, P + h)),
                  pl.BlockSpec((1, tk, W), lambda b, h, i, k: (b, k, h)),
                  pl.BlockSpec((1, tk, HEAD), lambda b, h, i, k: (b, k, 0)),
                  pl.BlockSpec((1, tk, W), lambda b, h, i, k: (b, k, P + h))],
        out_specs=pl.BlockSpec((1, tq, W), lambda b, h, i, k: (b, i, h)),
        out_shape=jax.ShapeDtypeStruct((B, nq, H * HEAD), BF16),
        scratch_shapes=[pltpu.VMEM((2, tq, 1), F32), pltpu.VMEM((2, tq, 1), F32),
                        pltpu.VMEM((2, tq, HEAD), F32), pltpu.VMEM((2, tk, W), BF16)],
        compiler_params=_params("parallel", "parallel", "parallel", "arbitrary"),
        name="mla_attn",
    )(q_src, q_src, kv_src, kr_src, kv_src)


def _mla_down_kernel(x_ref, sh_ref, sc_ref, w_ref, gq_ref, gkv_ref, *rest, rq, rkv, rope):
    if rope:
        cos_ref, sin_ref, cq_ref, ckv_ref, kr_ref = rest
    else:
        cq_ref, ckv_ref, kr_ref = rest
    h = (x_ref[0] * (1.0 + sc_ref[0]) + sh_ref[0]).astype(BF16)
    y = jnp.dot(h, w_ref[...], preferred_element_type=F32)

    def rms(t, g):
        return t * lax.rsqrt(jnp.mean(t * t, axis=1, keepdims=True) + RMS_EPS) * g

    cq_ref[0] = rms(y[:, :rq], gq_ref[...]).astype(BF16)
    ckv_ref[0] = rms(y[:, rq:rq + rkv], gkv_ref[...]).astype(BF16)
    kr = y[:, rq + rkv:]
    if rope:
        kr = _rope_chunk(kr, cos_ref[...], sin_ref[...], MLA_ROPE // 4)
    kr_ref[0] = kr.astype(BF16)


def _mla_down(x, w, gq, gkv, mod, *, ctx, rope):
    B, R, D = x.shape
    rq, rkv = gq.shape[0], gkv.shape[0]
    N = w.shape[1]
    tm = _tile(R, TM)
    row = _mod_row(ctx, B)
    in_specs = [pl.BlockSpec((1, tm, D), lambda b, i: (b, i, 0)),
                pl.BlockSpec((1, 1, D), lambda b, i: (row(b), 0, 0)),
                pl.BlockSpec((1, 1, D), lambda b, i: (row(b), 0, 0)),
                pl.BlockSpec((D, N), lambda b, i: (0, 0)),
                pl.BlockSpec((1, rq), lambda b, i: (0, 0)),
                pl.BlockSpec((1, rkv), lambda b, i: (0, 0))]
    args = [x, mod[0], mod[1], w, gq.reshape(1, rq), gkv.reshape(1, rkv)]
    if rope is not None:
        in_specs += [pl.BlockSpec((tm, LANE), lambda b, i: (i, 0))] * 2
        args += [rope[0], rope[1]]
    return pl.pallas_call(
        functools.partial(_mla_down_kernel, rq=rq, rkv=rkv, rope=rope is not None),
        grid=(B, R // tm),
        in_specs=in_specs,
        out_specs=[pl.BlockSpec((1, tm, rq), lambda b, i: (b, i, 0)),
                   pl.BlockSpec((1, tm, rkv), lambda b, i: (b, i, 0)),
                   pl.BlockSpec((1, tm, LANE), lambda b, i: (b, i, 0))],
        out_shape=[jax.ShapeDtypeStruct((B, R, rq), BF16),
                   jax.ShapeDtypeStruct((B, R, rkv), BF16),
                   jax.ShapeDtypeStruct((B, R, LANE), BF16)],
        compiler_params=_params("parallel", "parallel"),
        name="mla_down",
    )(*args)


def _sink_softmax_pv(s, sk, v):
    mx = jnp.maximum(jnp.max(s, axis=1, keepdims=True), sk)
    e = jnp.exp(s - mx)
    r = 1.0 / (jnp.sum(e, axis=1, keepdims=True) + jnp.exp(sk - mx))
    return jnp.dot((e * r).astype(BF16), v, preferred_element_type=F32)


def _win_attn_kernel(sink_ref, q_ref, kp_ref, kc_ref, kn_ref, kx_ref, vp_ref, vc_ref, vn_ref, vx_ref, o_ref,
                     *, tq, n, R):
    g = pl.program_id(1)
    qi = pl.program_id(2)
    nsub = tq // WINDOW
    L = 3 * WINDOW + kx_ref.shape[1]
    row = lax.broadcasted_iota(jnp.int32, (R * WINDOW, L), 0)
    col = lax.broadcasted_iota(jnp.int32, (R * WINDOW, L), 1)
    head = lax.broadcasted_iota(jnp.int32, (R * WINDOW, 1), 0) // WINDOW
    pos = row % WINDOW
    sk = jnp.zeros((R * WINDOW, 1), F32)
    for r in range(R):
        sk = jnp.where(head == r, sink_ref[g * R + r], sk)
    scale = HEAD ** -0.5

    def piece(p_ref, c_ref, n_ref, j):
        if j < 0:
            return p_ref[0]
        if j >= nsub:
            return n_ref[0]
        return c_ref[0, j * WINDOW:(j + 1) * WINDOW]

    for j in range(nsub):
        k = jnp.concatenate([piece(kp_ref, kc_ref, kn_ref, jj) for jj in (j - 1, j, j + 1)] + [kx_ref[0]], axis=0)
        v = jnp.concatenate([piece(vp_ref, vc_ref, vn_ref, jj) for jj in (j - 1, j, j + 1)] + [vx_ref[0]], axis=0)
        base = qi * tq + j * WINDOW
        key_pos = base - WINDOW + col
        valid = ((jnp.abs(base + pos - key_pos) <= WINDOW) & (key_pos >= 0) & (key_pos < n)) | (col >= 3 * WINDOW)
        rows = slice(j * WINDOW, (j + 1) * WINDOW)
        q = jnp.concatenate([q_ref[0, rows, r * HEAD:(r + 1) * HEAD] for r in range(R)], axis=0)
        s = jnp.where(valid, _dot_nt(q, k) * scale, NEG_INF)
        o = _sink_softmax_pv(s, sk, v).astype(o_ref.dtype)
        for r in range(R):
            o_ref[0, rows, r * HEAD:(r + 1) * HEAD] = o[r * WINDOW:(r + 1) * WINDOW]


def _win_attn(qkv, qkv_ctx, sink, H, G):
    B, n, _ = qkv.shape
    m = qkv_ctx.shape[1]
    R = H // G
    assert n % WINDOW == 0
    tq = _tile(n, TW, WINDOW)
    nsub = tq // WINDOW
    nb = n // tq
    nw = n // WINDOW
    prev = lambda i: jnp.maximum(i * nsub - 1, 0)
    nxt = lambda i: jnp.minimum((i + 1) * nsub, nw - 1)
    ident = lambda i: i
    kspec = lambda f, rows, off: pl.BlockSpec((1, rows, HEAD), lambda b, g, i: (b, f(i), off + g))
    xspec = lambda off: pl.BlockSpec((1, m, HEAD), lambda b, g, i: (b, 0, off + g))
    return pl.pallas_call(
        functools.partial(_win_attn_kernel, tq=tq, n=n, R=R),
        grid=(B, G, nb),
        in_specs=[pl.BlockSpec(memory_space=pltpu.SMEM),
                  pl.BlockSpec((1, tq, R * HEAD), lambda b, g, i: (b, i, g)),
                  kspec(prev, WINDOW, H), kspec(ident, tq, H), kspec(nxt, WINDOW, H), xspec(H),
                  kspec(prev, WINDOW, H + G), kspec(ident, tq, H + G), kspec(nxt, WINDOW, H + G), xspec(H + G)],
        out_specs=pl.BlockSpec((1, tq, R * HEAD), lambda b, g, i: (b, i, g)),
        out_shape=jax.ShapeDtypeStruct((B, n, H * HEAD), BF16),
        compiler_params=_params("parallel", "parallel", "parallel"),
        name="win_attn",
    )(sink.astype(F32), qkv, qkv, qkv, qkv, qkv_ctx, qkv, qkv, qkv, qkv_ctx)


def _sink_attn_kernel(sink_ref, q_ref, k_ref, v_ref, o_ref, *, R):
    g = pl.program_id(1)
    q, k, v = q_ref[0], k_ref[0], v_ref[0]
    scale = HEAD ** -0.5
    for r in range(R):
        sl = slice(r * HEAD, (r + 1) * HEAD)
        s = _dot_nt(q[:, sl], k) * scale
        o_ref[0, :, sl] = _sink_softmax_pv(s, sink_ref[g * R + r], v).astype(o_ref.dtype)


def _sink_attn(qkv, sink, H, G):
    B, m, _ = qkv.shape
    R = H // G
    return pl.pallas_call(
        functools.partial(_sink_attn_kernel, R=R),
        grid=(B, G),
        in_specs=[pl.BlockSpec(memory_space=pltpu.SMEM),
                  pl.BlockSpec((1, m, R * HEAD), lambda b, g: (b, 0, g)),
                  pl.BlockSpec((1, m, HEAD), lambda b, g: (b, 0, H + g)),
                  pl.BlockSpec((1, m, HEAD), lambda b, g: (b, 0, H + G + g))],
        out_specs=pl.BlockSpec((1, m, R * HEAD), lambda b, g: (b, 0, g)),
        out_shape=jax.ShapeDtypeStruct((B, m, H * HEAD), BF16),
        compiler_params=_params("parallel", "parallel"),
        name="sink_attn",
    )(sink.astype(F32), qkv, qkv, qkv)


def _silu(g):
    return g * jax.nn.sigmoid(g)


def _ffn_kernel(x_ref, sh_ref, sc_ref, gate_ref, w1_ref, w3_ref, w2_ref, g_ref, b_ref, o_ref, h_ref, acc_ref,
                *, alpha, nf):
    f = pl.program_id(2)

    @pl.when(f == 0)
    def _():
        h_ref[...] = (x_ref[0] * (1.0 + sc_ref[0]) + sh_ref[0]).astype(BF16)
        acc_ref[...] = jnp.zeros(acc_ref.shape, F32)

    h = h_ref[...]
    a = _silu(jnp.dot(h, w1_ref[0], preferred_element_type=F32)) * jnp.dot(h, w3_ref[0], preferred_element_type=F32)
    acc_ref[...] += jnp.dot(a.astype(BF16), w2_ref[0], preferred_element_type=F32)

    @pl.when(f == nf - 1)
    def _():
        o_ref[0] = _res_ln(x_ref[0], acc_ref[...], gate_ref[0], g_ref[...], b_ref[...], alpha)


def _ffn(x, w13, w2, slot, mod, g, b, *, ctx, alpha):
    B, R, D = x.shape
    Fd = w2.shape[1]
    tm = _tile(R, TM)
    tf = _tile(Fd, TF, LANE)
    nf = Fd // tf
    row = _mod_row(ctx, B)
    vec = pl.BlockSpec((1, 1, D), lambda b, i, f: (row(b), 0, 0))
    return pl.pallas_call(
        functools.partial(_ffn_kernel, alpha=alpha, nf=nf),
        grid=(B, R // tm, nf),
        in_specs=[pl.BlockSpec((1, tm, D), lambda b, i, f: (b, i, 0)), vec, vec, vec,
                  pl.BlockSpec((1, D, tf), lambda b, i, f: (slot, 0, f)),
                  pl.BlockSpec((1, D, tf), lambda b, i, f: (slot, 0, nf + f)),
                  pl.BlockSpec((1, tf, D), lambda b, i, f: (slot, f, 0)),
                  pl.BlockSpec((1, D), lambda b, i, f: (0, 0)),
                  pl.BlockSpec((1, D), lambda b, i, f: (0, 0))],
        out_specs=pl.BlockSpec((1, tm, D), lambda b, i, f: (b, i, 0)),
        out_shape=jax.ShapeDtypeStruct((B, R, D), F32),
        scratch_shapes=[pltpu.VMEM((tm, D), BF16), pltpu.VMEM((tm, D), F32)],
        compiler_params=_params("parallel", "parallel", "arbitrary"),
        name="ffn",
    )(x, mod[0], mod[1], mod[2], w13, w13, w2, g.reshape(1, D), b.reshape(1, D))


def _router_kernel(x_ref, sh_ref, sc_ref, rt_ref, h_ref, idx_ref, wts_ref):
    hf = x_ref[0] * (1.0 + sc_ref[0]) + sh_ref[0]
    h_ref[0] = hf
    logits = jnp.dot(hf, rt_ref[...], preferred_element_type=F32, precision=lax.Precision.HIGHEST)
    E = logits.shape[1]
    lane = lax.broadcasted_iota(jnp.int32, logits.shape, 1)
    m1 = jnp.max(logits, axis=1, keepdims=True)
    i1 = jnp.min(jnp.where(logits == m1, lane, E), axis=1, keepdims=True)
    rest = jnp.where(lane == i1, -jnp.inf, logits)
    m2 = jnp.max(rest, axis=1, keepdims=True)
    i2 = jnp.min(jnp.where(rest == m2, lane, E), axis=1, keepdims=True)
    w1 = 1.0 / (1.0 + jnp.exp(m2 - m1))
    two = lax.broadcasted_iota(jnp.int32, idx_ref.shape[1:], 1)
    idx_ref[0] = jnp.where(two == 0, i1, i2)
    wts_ref[0] = jnp.where(two == 0, w1, 1.0 - w1)


def _router(x, router, mod, *, ctx):
    B, R, D = x.shape
    E = router.shape[1]
    tm = _tile(R, TM)
    row = _mod_row(ctx, B)
    vec = pl.BlockSpec((1, 1, D), lambda b, i: (row(b), 0, 0))
    return pl.pallas_call(
        _router_kernel,
        grid=(B, R // tm),
        in_specs=[pl.BlockSpec((1, tm, D), lambda b, i: (b, i, 0)), vec, vec,
                  pl.BlockSpec((D, E), lambda b, i: (0, 0))],
        out_specs=[pl.BlockSpec((1, tm, D), lambda b, i: (b, i, 0)),
                   pl.BlockSpec((1, tm, 2), lambda b, i: (b, i, 0)),
                   pl.BlockSpec((1, tm, 2), lambda b, i: (b, i, 0))],
        out_shape=[jax.ShapeDtypeStruct((B, R, D), F32),
                   jax.ShapeDtypeStruct((B, R, 2), jnp.int32),
                   jax.ShapeDtypeStruct((B, R, 2), F32)],
        compiler_params=_params("parallel", "parallel"),
        name="router",
    )(x, mod[0], mod[1], router)


def _route_plan(idx, wts, E, tm):
    T = idx.shape[0]
    P = 2 * T
    e_flat = idx.reshape(P)
    onehot = (e_flat[:, None] == jnp.arange(E, dtype=jnp.int32)[None, :]).astype(jnp.int32)
    counts = jnp.sum(onehot, axis=0)
    rank = jnp.take_along_axis(jnp.cumsum(onehot, axis=0) - onehot, e_flat[:, None], axis=1)[:, 0]
    padded = ((counts + tm - 1) // tm) * tm
    ends = jnp.cumsum(padded)
    starts = ends - padded
    slot = starts[e_flat] + rank
    ntiles = P // tm + E
    S = ntiles * tm
    tile_start = jnp.arange(ntiles, dtype=jnp.int32) * tm
    tile_expert = jnp.minimum(jnp.searchsorted(ends, tile_start, side="right"), E - 1).astype(jnp.int32)
    tile_rows = jnp.clip(counts[tile_expert] - (tile_start - starts[tile_expert]), 0, tm)
    tile_rows = jnp.where(tile_start < ends[E - 1], tile_rows, 0).astype(jnp.int32)
    last = jnp.maximum(jnp.sum((tile_rows > 0).astype(jnp.int32)) - 1, 0)
    tile_expert = jnp.where(tile_rows > 0, tile_expert, tile_expert[last])
    pair = jnp.zeros((S,), jnp.int32).at[slot].set(jnp.arange(P, dtype=jnp.int32))
    token, choice = pair // 2, pair % 2
    src = token
    dst = choice * T + token
    gate = wts.reshape(P)[pair]
    return tile_expert, tile_rows, src.reshape(ntiles, 1, tm), dst.reshape(ntiles, 1, tm), gate.reshape(S, 1)


def _experts_kernel(te_ref, tr_ref, src_ref, dst_ref, gate_ref, h_hbm, w1_ref, w3_ref, w2_ref, y_hbm,
                    rows_ref, hb_ref, acc_ref, sem_in, sem_out, *, nf, tm):
    t = pl.program_id(0)
    f = pl.program_id(1)
    nrows = tr_ref[t]
    valid = nrows > 0

    def row_dmas(copy_row):
        ngroups = nrows // ROW_UNROLL

        def group(i, carry):
            for u in range(ROW_UNROLL):
                copy_row(i * ROW_UNROLL + u).start()
            return carry

        def single(r, carry):
            copy_row(r).start()
            return carry

        lax.fori_loop(0, ngroups, group, 0)
        lax.fori_loop(ngroups * ROW_UNROLL, nrows, single, 0)

    def wait_rows(copy_block, copy_row):
        nfull = pl.multiple_of((nrows // SUBLANE) * SUBLANE, SUBLANE)

        @pl.when(nfull > 0)
        def _():
            copy_block(nfull).wait()

        def single(r, carry):
            copy_row(r).wait()
            return carry

        lax.fori_loop(nfull, nrows, single, 0)

    @pl.when((t == 0) & (f == 0))
    def _():
        rows_ref[...] = jnp.zeros(rows_ref.shape, F32)

    @pl.when(valid & (f == 0))
    def _():
        gather_row = lambda r: pltpu.make_async_copy(h_hbm.at[pl.ds(src_ref[0, 0, r], 1)],
                                                     rows_ref.at[pl.ds(r, 1)], sem_in)
        row_dmas(gather_row)
        wait_rows(lambda n: pltpu.make_async_copy(h_hbm.at[pl.ds(0, n)], rows_ref.at[pl.ds(0, n)], sem_in),
                  gather_row)
        hb_ref[...] = rows_ref[...].astype(BF16)
        acc_ref[...] = jnp.zeros(acc_ref.shape, F32)

    @pl.when(valid)
    def _():
        h = hb_ref[...]
        a = _silu(jnp.dot(h, w1_ref[0], preferred_element_type=F32)) * jnp.dot(h, w3_ref[0], preferred_element_type=F32)
        acc_ref[...] += jnp.dot(a.astype(BF16), w2_ref[0], preferred_element_type=F32)

    @pl.when(valid & (f == nf - 1))
    def _():
        rows_ref[...] = acc_ref[...] * gate_ref[...]
        scatter_row = lambda r: pltpu.make_async_copy(rows_ref.at[pl.ds(r, 1)],
                                                      y_hbm.at[pl.ds(dst_ref[0, 0, r], 1)], sem_out)
        row_dmas(scatter_row)
        wait_rows(lambda n: pltpu.make_async_copy(rows_ref.at[pl.ds(0, n)], y_hbm.at[pl.ds(0, n)], sem_out),
                  scatter_row)


def _experts(h, plan, w13, w2, e_off, tm):
    T, D = h.shape
    Fd = w2.shape[1]
    tile_expert, tile_rows, src, dst, gate = plan
    tile_expert = tile_expert + e_off
    ntiles = src.shape[0]
    tf = _tile(Fd, TF, LANE)
    nf = Fd // tf
    fidx = lambda t, f, tv: jnp.where(tv[t] > 0, f, nf - 1)
    grid_spec = pltpu.PrefetchScalarGridSpec(
        num_scalar_prefetch=2,
        grid=(ntiles, nf),
        in_specs=[pl.BlockSpec((1, 1, tm), lambda t, f, te, tv: (t, 0, 0), memory_space=pltpu.SMEM),
                  pl.BlockSpec((1, 1, tm), lambda t, f, te, tv: (t, 0, 0), memory_space=pltpu.SMEM),
                  pl.BlockSpec((tm, 1), lambda t, f, te, tv: (t, 0)),
                  pl.BlockSpec(memory_space=pl.ANY),
                  pl.BlockSpec((1, D, tf), lambda t, f, te, tv: (te[t], 0, fidx(t, f, tv))),
                  pl.BlockSpec((1, D, tf), lambda t, f, te, tv: (te[t], 0, nf + fidx(t, f, tv))),
                  pl.BlockSpec((1, tf, D), lambda t, f, te, tv: (te[t], fidx(t, f, tv), 0))],
        out_specs=pl.BlockSpec(memory_space=pl.ANY),
        scratch_shapes=[pltpu.VMEM((tm, D), F32), pltpu.VMEM((tm, D), BF16), pltpu.VMEM((tm, D), F32),
                        pltpu.SemaphoreType.DMA(()), pltpu.SemaphoreType.DMA(())],
    )
    return pl.pallas_call(
        functools.partial(_experts_kernel, nf=nf, tm=tm),
        grid_spec=grid_spec,
        out_shape=jax.ShapeDtypeStruct((2 * T, D), F32),
        compiler_params=pltpu.CompilerParams(dimension_semantics=("arbitrary", "arbitrary"),
                                             vmem_limit_bytes=VMEM_LIMIT),
        name="experts",
    )(tile_expert, tile_rows, src, dst, gate, h, w13, w13, w2)


def _combine_kernel(x_ref, y0_ref, y1_ref, gate_ref, g_ref, b_ref, o_ref, *, alpha):
    o_ref[0] = _res_ln(x_ref[0], y0_ref[...] + y1_ref[...], gate_ref[0], g_ref[...], b_ref[...], alpha)


def _combine_ln(x, y, gate, g, b, *, ctx, alpha):
    B, R, D = x.shape
    tm = _tile(R, TM)
    nb = R // tm
    row = _mod_row(ctx, B)
    return pl.pallas_call(
        functools.partial(_combine_kernel, alpha=alpha),
        grid=(B, nb),
        in_specs=[pl.BlockSpec((1, tm, D), lambda b, i: (b, i, 0)),
                  pl.BlockSpec((tm, D), lambda b, i: (b * nb + i, 0)),
                  pl.BlockSpec((tm, D), lambda b, i: (B * nb + b * nb + i, 0)),
                  pl.BlockSpec((1, 1, D), lambda b, i: (row(b), 0, 0)),
                  pl.BlockSpec((1, D), lambda b, i: (0, 0)),
                  pl.BlockSpec((1, D), lambda b, i: (0, 0))],
        out_specs=pl.BlockSpec((1, tm, D), lambda b, i: (b, i, 0)),
        out_shape=jax.ShapeDtypeStruct((B, R, D), F32),
        compiler_params=_params("parallel", "parallel"),
        name="combine_ln",
    )(x, y, y, gate, g.reshape(1, D), b.reshape(1, D))


def _moe(x, router, w13, w2, slot, mod, g, b, *, ctx, alpha):
    B, R, D = x.shape
    E = router.shape[1]
    T = B * R
    tm = _tile(2 * T, TME)
    h, idx, wts = _router(x, router, mod, ctx=ctx)
    plan = _route_plan(idx.reshape(T, 2), wts.reshape(T, 2), E, tm)
    y = _experts(h.reshape(T, D), plan, w13, w2, slot * E, tm)
    return _combine_ln(x, y, mod[2], g, b, ctx=ctx, alpha=alpha)


def _mix_diff(xl, xc, mod, need_ctx, layer, wqkv, lam, subln_g, rope128):
    D = xl.shape[2]
    H = D // (2 * HEAD)
    lam_init = 0.8 - 0.6 * math.exp(-0.3 * layer)
    w = wqkv.astype(BF16)
    qs = HEAD ** -0.5 * LOG2E
    qkv_l = _proj(xl, w, mod=mod, ctx=False, rope=rope128, rope_cols=(0, 2 * D), q_cols=D, q_scale=qs)
    qkv_c = _proj(xc, w, mod=mod, ctx=True, q_cols=D, q_scale=qs)
    kv_all = jnp.concatenate([qkv_l, qkv_c], axis=1)
    o_l = _diff_attn(qkv_l, kv_all, lam, subln_g, lam_init, H)
    o_c = _diff_attn(qkv_c, qkv_c, lam, subln_g, lam_init, H) if need_ctx else None
    return o_l, o_c


def _mix_window(xl, xc, mod, need_ctx, wqkv, sink, rope128):
    D = xl.shape[2]
    H = D // HEAD
    G = H // 4
    w = wqkv.astype(BF16)
    qkv_l = _proj(xl, w, mod=mod, ctx=False, rope=rope128, rope_cols=(0, (H + G) * HEAD))
    qkv_c = _proj(xc, w, mod=mod, ctx=True)
    o_l = _win_attn(qkv_l, qkv_c, sink, H, G)
    o_c = _sink_attn(qkv_c, sink, H, G) if need_ctx else None
    return o_l, o_c


def _mix_mla(xl, xc, mod, need_ctx, w_down, gq, gkv, w_uq, w_ukv, rope64):
    D = xl.shape[2]
    H = D // HEAD
    rq, rkv = gq.shape[0], gkv.shape[0]
    wd = jnp.pad(w_down, ((0, 0), (0, LANE - MLA_ROPE))).astype(BF16)
    wq3 = w_uq.reshape(rq, H, HEAD + MLA_ROPE)
    wq = jnp.concatenate(
        [wq3[:, :, :HEAD].reshape(rq, H * HEAD),
         jnp.pad(wq3[:, :, HEAD:], ((0, 0), (0, 0), (0, HEAD - MLA_ROPE))).reshape(rq, H * HEAD)], axis=1).astype(BF16)
    wkv3 = w_ukv.reshape(rkv, H, 2 * HEAD)
    wkv = jnp.concatenate([wkv3[:, :, :HEAD].reshape(rkv, H * HEAD),
                           wkv3[:, :, HEAD:].reshape(rkv, H * HEAD)], axis=1).astype(BF16)
    cq_l, ckv_l, kr_l = _mla_down(xl, wd, gq, gkv, mod, ctx=False, rope=rope64)
    cq_c, ckv_c, kr_c = _mla_down(xc, wd, gq, gkv, mod, ctx=True, rope=None)
    qs = (HEAD + MLA_ROPE) ** -0.5 * LOG2E
    q_l = _proj(cq_l, wq, rope=rope64, rope_cols=(H * HEAD, 2 * H * HEAD), q_cols=2 * H * HEAD, q_scale=qs)
    kv_l = _proj(ckv_l, wkv)
    kv_c = _proj(ckv_c, wkv)
    kv_all = jnp.concatenate([kv_l, kv_c], axis=1)
    kr_all = jnp.concatenate([kr_l, kr_c], axis=1)
    o_l = _mla_attn(q_l, kv_all, kr_all, H)
    o_c = None
    if need_ctx:
        q_c = _proj(cq_c, wq, q_cols=2 * H * HEAD, q_scale=qs)
        o_c = _mla_attn(q_c, kv_c, kr_c, H)
    return o_l, o_c


def kernel(x, c, ctx, c_ctx, mod_w, mod_b, ln1_g, ln1_b, ln2_g, ln2_b, da_wqkv, da_lambda, da_subln_g, da_wo, wa_wqkv, wa_sink, wa_wo, mla_w_down, mla_q_norm_g, mla_kv_norm_g, mla_w_uq, mla_w_ukv, mla_wo, ffn_w13, ffn_w2, moe_router, moe_w13, moe_w2):
    B, n, D = x.shape
    depth = mod_w.shape[0]
    alpha = (2.0 * depth) ** 0.25
    rows = SUBLANE * (-(-(B + 1) // SUBLANE))
    cc = jnp.concatenate([c, c_ctx[None, :], jnp.zeros((rows - B - 1, D), F32)], axis=0)
    mod_all = _modulation(cc, mod_w, mod_b)
    rope128 = _rope_tables(n, HEAD)
    rope64 = _rope_tables(n, MLA_ROPE)
    ffn_w13_b, ffn_w2_b = ffn_w13.astype(BF16), ffn_w2.astype(BF16)
    moe_w13_b = moe_w13.astype(BF16).reshape((-1,) + moe_w13.shape[2:])
    moe_w2_b = moe_w2.astype(BF16).reshape((-1,) + moe_w2.shape[2:])
    xl, xc = x, ctx
    for i in range(depth):
        need_ctx = i < depth - 1
        mod = [mod_all[i, :, k * D:(k + 1) * D].reshape(rows, 1, D) for k in range(6)]
        kind, slot = i % N_MIXERS, i // N_MIXERS
        if kind == 0:
            o_l, o_c = _mix_diff(xl, xc, mod[0:2], need_ctx, i, da_wqkv[slot], da_lambda[slot],
                                 da_subln_g[slot], rope128)
            wo = da_wo[slot]
        elif kind == 1:
            o_l, o_c = _mix_window(xl, xc, mod[0:2], need_ctx, wa_wqkv[slot], wa_sink[slot], rope128)
            wo = wa_wo[slot]
        else:
            o_l, o_c = _mix_mla(xl, xc, mod[0:2], need_ctx, mla_w_down[slot], mla_q_norm_g[slot],
                                mla_kv_norm_g[slot], mla_w_uq[slot], mla_w_ukv[slot], rope64)
            wo = mla_wo[slot]
        wo = wo.astype(BF16)
        xl = _oproj_ln(xl, o_l, wo, mod[2], ln1_g[i], ln1_b[i], ctx=False, alpha=alpha)
        if need_ctx:
            xc = _oproj_ln(xc, o_c, wo, mod[2], ln1_g[i], ln1_b[i], ctx=True, alpha=alpha)
        cslot = i // 2
        if i % 2 == 0:
            xl = _ffn(xl, ffn_w13_b, ffn_w2_b, cslot, mod[3:6], ln2_g[i], ln2_b[i], ctx=False, alpha=alpha)
            if need_ctx:
                xc = _ffn(xc, ffn_w13_b, ffn_w2_b, cslot, mod[3:6], ln2_g[i], ln2_b[i], ctx=True, alpha=alpha)
        else:
            xl = _moe(xl, moe_router[cslot], moe_w13_b, moe_w2_b, cslot, mod[3:6], ln2_g[i], ln2_b[i],
                      ctx=False, alpha=alpha)
            if need_ctx:
                xc = _moe(xc, moe_router[cslot], moe_w13_b, moe_w2_b, cslot, mod[3:6], ln2_g[i], ln2_b[i],
                          ctx=True, alpha=alpha)
    return xl
```

```python
import functools
import math

import jax
import jax.numpy as jnp
from jax import lax
from jax.experimental import pallas as pl
from jax.experimental.pallas import tpu as pltpu

F32 = jnp.float32
BF16 = jnp.bfloat16

LANE = 128
SUBLANE = 8
VMEM_LIMIT = 56 * 1024 * 1024

GRID_W = 64
ROPE_BASE = 10000.0
LN_EPS = 1e-5
RMS_EPS = 1e-6
NEG_INF = -1e30
HEAD = 128
WINDOW = 128
MLA_ROPE = 64
N_MIXERS = 3
LOG2E = math.log2(math.e)

TM = 512
TMP = 1024
TW = 512
TN = 1024
TF = 512
TQ = 1024
TK = 2816
RB = 256
TME = 1024
ROW_UNROLL = 8


def _tile(dim, pref, align=SUBLANE):
    if dim <= pref:
        return dim
    t = (pref // align) * align
    while t >= align:
        if dim % t == 0:
            return t
        t -= align
    return dim


def _params(*sem):
    return pltpu.CompilerParams(dimension_semantics=sem, vmem_limit_bytes=VMEM_LIMIT)


def _mod_row(ctx, nb):
    return (lambda b: nb) if ctx else (lambda b: b)


def _mod_kernel(s_ref, w_ref, b_ref, o_ref):
    s = s_ref[...]
    s = s * jax.nn.sigmoid(s)
    o_ref[0] = jnp.dot(s.astype(BF16), w_ref[0].astype(BF16), preferred_element_type=F32) + b_ref[0]


def _modulation(cc, mod_w, mod_b):
    depth, D, N = mod_w.shape
    R = cc.shape[0]
    tn = _tile(N, 1024, LANE)
    return pl.pallas_call(
        _mod_kernel,
        grid=(depth, N // tn),
        in_specs=[pl.BlockSpec((R, D), lambda l, j: (0, 0)),
                  pl.BlockSpec((1, D, tn), lambda l, j: (l, 0, j)),
                  pl.BlockSpec((1, 1, tn), lambda l, j: (l, 0, j))],
        out_specs=pl.BlockSpec((1, R, tn), lambda l, j: (l, 0, j)),
        out_shape=jax.ShapeDtypeStruct((depth, R, N), F32),
        compiler_params=_params("parallel", "parallel"),
        name="modulation",
    )(cc, mod_w, mod_b.reshape(depth, 1, N))


def _rope_tables(n, dim):
    half = dim // 4
    t = jnp.arange(n)
    row = (t // GRID_W).astype(F32)
    col = (t % GRID_W).astype(F32)
    inv = ROPE_BASE ** (-jnp.arange(half, dtype=F32) / half)
    ar = row[:, None] * inv[None, :]
    ac = col[:, None] * inv[None, :]
    cos = jnp.concatenate([jnp.cos(ar), jnp.cos(ar), jnp.cos(ac), jnp.cos(ac)], axis=1)
    sin = jnp.concatenate([-jnp.sin(ar), jnp.sin(ar), -jnp.sin(ac), jnp.sin(ac)], axis=1)
    rep = LANE // dim
    return jnp.tile(cos, (1, rep)), jnp.tile(sin, (1, rep)), half


def _rope_chunk(yc, cos, sin, half):
    lane = lax.broadcasted_iota(jnp.int32, (1, LANE), 1)
    first = (lane % (2 * half)) < half
    rot = jnp.where(first, pltpu.roll(yc, LANE - half, 1), pltpu.roll(yc, half, 1))
    return yc * cos + rot * sin


def _proj_kernel(*refs, modulate, rope_lo, rope_hi, half, q_hi, q_scale):
    refs = list(refs)
    x_ref = refs.pop(0)
    if modulate:
        sh_ref, sc_ref = refs.pop(0), refs.pop(0)
    w_ref = refs.pop(0)
    if rope_hi > rope_lo:
        cos_ref, sin_ref = refs.pop(0), refs.pop(0)
    o_ref = refs.pop(0)
    j = pl.program_id(2)
    if modulate:
        h_ref = refs.pop(0)

        @pl.when(j == 0)
        def _():
            h_ref[...] = (x_ref[0] * (1.0 + sc_ref[0]) + sh_ref[0]).astype(BF16)

        h = h_ref[...]
    else:
        h = x_ref[0]
    y = jnp.dot(h, w_ref[...], preferred_element_type=F32)
    if q_hi > 0:
        y = y * jnp.where(j < q_hi, q_scale, 1.0)
    if rope_hi > rope_lo:
        in_rope = (j >= rope_lo) & (j < rope_hi)

        @pl.when(in_rope)
        def _():
            cos, sin = cos_ref[...], sin_ref[...]
            for cidx in range(y.shape[1] // LANE):
                sl = slice(cidx * LANE, (cidx + 1) * LANE)
                o_ref[0, :, sl] = _rope_chunk(y[:, sl], cos, sin, half).astype(o_ref.dtype)

        @pl.when(jnp.logical_not(in_rope))
        def _():
            o_ref[0] = y.astype(o_ref.dtype)
    else:
        o_ref[0] = y.astype(o_ref.dtype)


def _proj(x, w, *, mod=None, ctx=False, rope=None, rope_cols=(0, 0), q_cols=0, q_scale=1.0):
    B, R, K = x.shape
    N = w.shape[1]
    tm = _tile(R, TMP)
    g = math.gcd(N, q_cols)
    if rope is not None:
        g = math.gcd(g, math.gcd(rope_cols[0], rope_cols[1]))
    tn = _tile(g, TN, LANE)
    modulate = mod is not None
    args = [x]
    in_specs = [pl.BlockSpec((1, tm, K), lambda b, i, j: (b, i, 0))]
    if modulate:
        shift, scale = mod
        row = _mod_row(ctx, B)
        in_specs += [pl.BlockSpec((1, 1, K), lambda b, i, j: (row(b), 0, 0))] * 2
        args += [shift, scale]
    in_specs.append(pl.BlockSpec((K, tn), lambda b, i, j: (0, j)))
    args.append(w)
    half = 0
    if rope is not None:
        cos, sin, half = rope
        in_specs += [pl.BlockSpec((tm, LANE), lambda b, i, j: (i, 0))] * 2
        args += [cos, sin]
    kern = functools.partial(_proj_kernel, modulate=modulate, rope_lo=rope_cols[0] // tn,
                             rope_hi=rope_cols[1] // tn if rope is not None else 0, half=half,
                             q_hi=q_cols // tn, q_scale=q_scale)
    return pl.pallas_call(
        kern,
        grid=(B, R // tm, N // tn),
        in_specs=in_specs,
        out_specs=pl.BlockSpec((1, tm, tn), lambda b, i, j: (b, i, j)),
        out_shape=jax.ShapeDtypeStruct((B, R, N), BF16),
        scratch_shapes=[pltpu.VMEM((tm, K), BF16)] if modulate else [],
        compiler_params=_params("parallel", "parallel", "arbitrary"),
        name="proj",
    )(*args)


def _res_ln(x, y, gate, g, b, alpha):
    z = alpha * x + gate * y
    mu = jnp.mean(z, axis=-1, keepdims=True)
    zc = z - mu
    var = jnp.mean(zc * zc, axis=-1, keepdims=True)
    return zc * lax.rsqrt(var + LN_EPS) * g + b


def _oproj_kernel(x_ref, o_ref, w_ref, gate_ref, g_ref, b_ref, out_ref, *, alpha):
    y = jnp.dot(o_ref[0], w_ref[...], preferred_element_type=F32)
    out_ref[0] = _res_ln(x_ref[0], y, gate_ref[0], g_ref[...], b_ref[...], alpha)


def _oproj_ln(x, o, w, gate, g, b, *, ctx, alpha):
    B, R, D = x.shape
    K = o.shape[2]
    tm = _tile(R, TM)
    row = _mod_row(ctx, B)
    return pl.pallas_call(
        functools.partial(_oproj_kernel, alpha=alpha),
        grid=(B, R // tm),
        in_specs=[pl.BlockSpec((1, tm, D), lambda b, i: (b, i, 0)),
                  pl.BlockSpec((1, tm, K), lambda b, i: (b, i, 0)),
                  pl.BlockSpec((K, D), lambda b, i: (0, 0)),
                  pl.BlockSpec((1, 1, D), lambda b, i: (row(b), 0, 0)),
                  pl.BlockSpec((1, D), lambda b, i: (0, 0)),
                  pl.BlockSpec((1, D), lambda b, i: (0, 0))],
        out_specs=pl.BlockSpec((1, tm, D), lambda b, i: (b, i, 0)),
        out_shape=jax.ShapeDtypeStruct((B, R, D), F32),
        compiler_params=_params("parallel", "parallel"),
        name="oproj_ln",
    )(x, o, w, gate, g.reshape(1, D), b.reshape(1, D))


def _flash_init(m_ref, l_ref, acc_ref):
    m_ref[...] = jnp.full(m_ref.shape, NEG_INF, F32)
    l_ref[...] = jnp.zeros(l_ref.shape, F32)
    acc_ref[...] = jnp.zeros(acc_ref.shape, F32)


def _dot_nt(a, b):
    return lax.dot_general(a, b, (((1,), (1,)), ((), ())), preferred_element_type=F32)


def _flash_step(q_fn, k_fn, v_fn, m_ref, l_ref, acc_ref, *, chains, tq):
    rb = min(RB, tq)

    def body(r, carry):
        rows = pl.ds(pl.multiple_of(r * rb, rb), rb)
        for c in range(chains):
            s = _dot_nt(q_fn(c, rows), k_fn(c))
            m_prev = m_ref[c, rows]
            m_new = jnp.maximum(m_prev, jnp.max(s, axis=1, keepdims=True))
            a = jnp.exp2(m_prev - m_new)
            p = jnp.exp2(s - m_new)
            l_ref[c, rows] = a * l_ref[c, rows] + jnp.sum(p, axis=1, keepdims=True)
            acc_ref[c, rows] = a * acc_ref[c, rows] + jnp.dot(p.astype(BF16), v_fn(c), preferred_element_type=F32)
            m_ref[c, rows] = m_new
        return carry

    lax.fori_loop(0, tq // rb, body, 0, unroll=4)


def _diff_attn_kernel(lam_ref, g_ref, q_ref, k_ref, v_ref, o_ref, m_ref, l_ref, acc_ref, *, lam_init, nk):
    ki = pl.program_id(3)

    @pl.when(ki == 0)
    def _():
        _flash_init(m_ref, l_ref, acc_ref)

    _flash_step(lambda c, rows: q_ref[0, rows, c * HEAD:(c + 1) * HEAD],
                lambda c: k_ref[0, :, c * HEAD:(c + 1) * HEAD],
                lambda c: v_ref[0],
                m_ref, l_ref, acc_ref, chains=2, tq=q_ref.shape[1])

    @pl.when(ki == nk - 1)
    def _():
        lam = lam_ref[...]
        lam_full = (jnp.exp(jnp.sum(lam[0:1] * lam[1:2], axis=1, keepdims=True))
                    - jnp.exp(jnp.sum(lam[2:3] * lam[3:4], axis=1, keepdims=True)) + lam_init)
        o = acc_ref[0] / l_ref[0] - lam_full * (acc_ref[1] / l_ref[1])
        ms = jnp.mean(o * o, axis=1, keepdims=True)
        o_ref[0] = (o * lax.rsqrt(ms + RMS_EPS) * g_ref[...] * (1.0 - lam_init)).astype(o_ref.dtype)


def _diff_attn(q_src, kv_src, lam, subln_g, lam_init, H):
    B, nq, _ = q_src.shape
    nkv = kv_src.shape[1]
    W = 2 * HEAD
    tq = _tile(nq, TQ)
    tk = _tile(nkv, TK)
    nk = nkv // tk
    return pl.pallas_call(
        functools.partial(_diff_attn_kernel, lam_init=lam_init, nk=nk),
        grid=(B, H, nq // tq, nk),
        in_specs=[pl.BlockSpec((4, HEAD), lambda b, h, i, k: (0, 0)),
                  pl.BlockSpec((1, W), lambda b, h, i, k: (0, 0)),
                  pl.BlockSpec((1, tq, W), lambda b, h, i, k: (b, i, h)),
                  pl.BlockSpec((1, tk, W), lambda b, h, i, k: (b, k, H + h)),
                  pl.BlockSpec((1, tk, W), lambda b, h, i, k: (b, k, 2 * H + h))],
        out_specs=pl.BlockSpec((1, tq, W), lambda b, h, i, k: (b, i, h)),
        out_shape=jax.ShapeDtypeStruct((B, nq, H * W), BF16),
        scratch_shapes=[pltpu.VMEM((2, tq, 1), F32), pltpu.VMEM((2, tq, 1), F32),
                        pltpu.VMEM((2, tq, W), F32)],
        compiler_params=_params("parallel", "parallel", "parallel", "arbitrary"),
        name="diff_attn",
    )(lam.astype(F32), subln_g.reshape(1, W), q_src, kv_src, kv_src)


def _mla_attn_kernel(qn_ref, qr_ref, kn_ref, kr_ref, v_ref, o_ref, m_ref, l_ref, acc_ref, k_ref, *, nk):
    ki = pl.program_id(3)

    @pl.when(ki == 0)
    def _():
        _flash_init(m_ref, l_ref, acc_ref)

    for c in range(2):
        k_ref[c, :, :HEAD] = kn_ref[0, :, c * HEAD:(c + 1) * HEAD]
        k_ref[c, :, HEAD:] = kr_ref[0]

    def q_fn(c, rows):
        sl = slice(c * HEAD, (c + 1) * HEAD)
        return jnp.concatenate([qn_ref[0, rows, sl], qr_ref[0, rows, sl]], axis=1)

    _flash_step(q_fn, lambda c: k_ref[c], lambda c: v_ref[0, :, c * HEAD:(c + 1) * HEAD],
                m_ref, l_ref, acc_ref, chains=2, tq=qn_ref.shape[1])

    @pl.when(ki == nk - 1)
    def _():
        for c in range(2):
            o_ref[0, :, c * HEAD:(c + 1) * HEAD] = (acc_ref[c] / l_ref[c]).astype(o_ref.dtype)


def _mla_attn(q_src, kv_src, kr_src, H):
    B, nq, _ = q_src.shape
    nkv = kv_src.shape[1]
    assert H % 2 == 0
    P = H // 2
    W = 2 * HEAD
    tq = _tile(nq, TQ)
    tk = _tile(nkv, TK)
    nk = nkv // tk
    return pl.pallas_call(
        functools.partial(_mla_attn_kernel, nk=nk),
        grid=(B, P, nq // tq, nk),
        in_specs=[pl.BlockSpec((1, tq, W), lambda b, h, i, k: (b, i, h)),
                  pl.BlockSpec((1, tq, W), lambda b, h, i, k: (b, i, P + h)),
                  pl.BlockSpec((1, tk, W), lambda b, h, i, k: (b, k, h)),
                  pl.BlockSpec((1, tk, HEAD), lambda b, h, i, k: (b, k, 0)),
                  pl.BlockSpec((1, tk, W), lambda b, h, i, k: (b, k, P + h))],
        out_specs=pl.BlockSpec((1, tq, W), lambda b, h, i, k: (b, i, h)),
        out_shape=jax.ShapeDtypeStruct((B, nq, H * HEAD), BF16),
        scratch_shapes=[pltpu.VMEM((2, tq, 1), F32), pltpu.VMEM((2, tq, 1), F32),
                        pltpu.VMEM((2, tq, HEAD), F32), pltpu.VMEM((2, tk, W), BF16)],
        compiler_params=_params("parallel", "parallel", "parallel", "arbitrary"),
        name="mla_attn",
    )(q_src, q_src, kv_src, kr_src, kv_src)


def _mla_down_kernel(x_ref, sh_ref, sc_ref, w_ref, gq_ref, gkv_ref, *rest, rq, rkv, rope):
    if rope:
        cos_ref, sin_ref, cq_ref, ckv_ref, kr_ref = rest
    else:
        cq_ref, ckv_ref, kr_ref = rest
    h = (x_ref[0] * (1.0 + sc_ref[0]) + sh_ref[0]).astype(BF16)
    y = jnp.dot(h, w_ref[...], preferred_element_type=F32)

    def rms(t, g):
        return t * lax.rsqrt(jnp.mean(t * t, axis=1, keepdims=True) + RMS_EPS) * g

    cq_ref[0] = rms(y[:, :rq], gq_ref[...]).astype(BF16)
    ckv_ref[0] = rms(y[:, rq:rq + rkv], gkv_ref[...]).astype(BF16)
    kr = y[:, rq + rkv:]
    if rope:
        kr = _rope_chunk(kr, cos_ref[...], sin_ref[...], MLA_ROPE // 4)
    kr_ref[0] = kr.astype(BF16)


def _mla_down(x, w, gq, gkv, mod, *, ctx, rope):
    B, R, D = x.shape
    rq, rkv = gq.shape[0], gkv.shape[0]
    N = w.shape[1]
    tm = _tile(R, TM)
    row = _mod_row(ctx, B)
    in_specs = [pl.BlockSpec((1, tm, D), lambda b, i: (b, i, 0)),
                pl.BlockSpec((1, 1, D), lambda b, i: (row(b), 0, 0)),
                pl.BlockSpec((1, 1, D), lambda b, i: (row(b), 0, 0)),
                pl.BlockSpec((D, N), lambda b, i: (0, 0)),
                pl.BlockSpec((1, rq), lambda b, i: (0, 0)),
                pl.BlockSpec((1, rkv), lambda b, i: (0, 0))]
    args = [x, mod[0], mod[1], w, gq.reshape(1, rq), gkv.reshape(1, rkv)]
    if rope is not None:
        in_specs += [pl.BlockSpec((tm, LANE), lambda b, i: (i, 0))] * 2
        args += [rope[0], rope[1]]
    return pl.pallas_call(
        functools.partial(_mla_down_kernel, rq=rq, rkv=rkv, rope=rope is not None),
        grid=(B, R // tm),
        in_specs=in_specs,
        out_specs=[pl.BlockSpec((1, tm, rq), lambda b, i: (b, i, 0)),
                   pl.BlockSpec((1, tm, rkv), lambda b, i: (b, i, 0)),
                   pl.BlockSpec((1, tm, LANE), lambda b, i: (b, i, 0))],
        out_shape=[jax.ShapeDtypeStruct((B, R, rq), BF16),
                   jax.ShapeDtypeStruct((B, R, rkv), BF16),
                   jax.ShapeDtypeStruct((B, R, LANE), BF16)],
        compiler_params=_params("parallel", "parallel"),
        name="mla_down",
    )(*args)


def _sink_softmax_pv(s, sk, v):
    mx = jnp.maximum(jnp.max(s, axis=1, keepdims=True), sk)
    e = jnp.exp(s - mx)
    r = 1.0 / (jnp.sum(e, axis=1, keepdims=True) + jnp.exp(sk - mx))
    return jnp.dot((e * r).astype(BF16), v, preferred_element_type=F32)


def _win_attn_kernel(sink_ref, q_ref, kp_ref, kc_ref, kn_ref, kx_ref, vp_ref, vc_ref, vn_ref, vx_ref, o_ref,
                     *, tq, n, R):
    g = pl.program_id(1)
    qi = pl.program_id(2)
    nsub = tq // WINDOW
    L = 3 * WINDOW + kx_ref.shape[1]
    row = lax.broadcasted_iota(jnp.int32, (R * WINDOW, L), 0)
    col = lax.broadcasted_iota(jnp.int32, (R * WINDOW, L), 1)
    head = lax.broadcasted_iota(jnp.int32, (R * WINDOW, 1), 0) // WINDOW
    pos = row % WINDOW
    sk = jnp.zeros((R * WINDOW, 1), F32)
    for r in range(R):
        sk = jnp.where(head == r, sink_ref[g * R + r], sk)
    scale = HEAD ** -0.5

    def piece(p_ref, c_ref, n_ref, j):
        if j < 0:
            return p_ref[0]
        if j >= nsub:
            return n_ref[0]
        return c_ref[0, j * WINDOW:(j + 1) * WINDOW]

    for j in range(nsub):
        k = jnp.concatenate([piece(kp_ref, kc_ref, kn_ref, jj) for jj in (j - 1, j, j + 1)] + [kx_ref[0]], axis=0)
        v = jnp.concatenate([piece(vp_ref, vc_ref, vn_ref, jj) for jj in (j - 1, j, j + 1)] + [vx_ref[0]], axis=0)
        base = qi * tq + j * WINDOW
        key_pos = base - WINDOW + col
        valid = ((jnp.abs(base + pos - key_pos) <= WINDOW) & (key_pos >= 0) & (key_pos < n)) | (col >= 3 * WINDOW)
        rows = slice(j * WINDOW, (j + 1) * WINDOW)
        q = jnp.concatenate([q_ref[0, rows, r * HEAD:(r + 1) * HEAD] for r in range(R)], axis=0)
        s = jnp.where(valid, _dot_nt(q, k) * scale, NEG_INF)
        o = _sink_softmax_pv(s, sk, v).astype(o_ref.dtype)
        for r in range(R):
            o_ref[0, rows, r * HEAD:(r + 1) * HEAD] = o[r * WINDOW:(r + 1) * WINDOW]


def _win_attn(qkv, qkv_ctx, sink, H, G):
    B, n, _ = qkv.shape
    m = qkv_ctx.shape[1]
    R = H // G
    assert n % WINDOW == 0
    tq = _tile(n, TW, WINDOW)
    nsub = tq // WINDOW
    nb = n // tq
    nw = n // WINDOW
    prev = lambda i: jnp.maximum(i * nsub - 1, 0)
    nxt = lambda i: jnp.minimum((i + 1) * nsub, nw - 1)
    ident = lambda i: i
    kspec = lambda f, rows, off: pl.BlockSpec((1, rows, HEAD), lambda b, g, i: (b, f(i), off + g))
    xspec = lambda off: pl.BlockSpec((1, m, HEAD), lambda b, g, i: (b, 0, off + g))
    return pl.pallas_call(
        functools.partial(_win_attn_kernel, tq=tq, n=n, R=R),
        grid=(B, G, nb),
        in_specs=[pl.BlockSpec(memory_space=pltpu.SMEM),
                  pl.BlockSpec((1, tq, R * HEAD), lambda b, g, i: (b, i, g)),
                  kspec(prev, WINDOW, H), kspec(ident, tq, H), kspec(nxt, WINDOW, H), xspec(H),
                  kspec(prev, WINDOW, H + G), kspec(ident, tq, H + G), kspec(nxt, WINDOW, H + G), xspec(H + G)],
        out_specs=pl.BlockSpec((1, tq, R * HEAD), lambda b, g, i: (b, i, g)),
        out_shape=jax.ShapeDtypeStruct((B, n, H * HEAD), BF16),
        compiler_params=_params("parallel", "parallel", "parallel"),
        name="win_attn",
    )(sink.astype(F32), qkv, qkv, qkv, qkv, qkv_ctx, qkv, qkv, qkv, qkv_ctx)


def _sink_attn_kernel(sink_ref, q_ref, k_ref, v_ref, o_ref, *, R):
    g = pl.program_id(1)
    q, k, v = q_ref[0], k_ref[0], v_ref[0]
    scale = HEAD ** -0.5
    for r in range(R):
        sl = slice(r * HEAD, (r + 1) * HEAD)
        s = _dot_nt(q[:, sl], k) * scale
        o_ref[0, :, sl] = _sink_softmax_pv(s, sink_ref[g * R + r], v).astype(o_ref.dtype)


def _sink_attn(qkv, sink, H, G):
    B, m, _ = qkv.shape
    R = H // G
    return pl.pallas_call(
        functools.partial(_sink_attn_kernel, R=R),
        grid=(B, G),
        in_specs=[pl.BlockSpec(memory_space=pltpu.SMEM),
                  pl.BlockSpec((1, m, R * HEAD), lambda b, g: (b, 0, g)),
                  pl.BlockSpec((1, m, HEAD), lambda b, g: (b, 0, H + g)),
                  pl.BlockSpec((1, m, HEAD), lambda b, g: (b, 0, H + G + g))],
        out_specs=pl.BlockSpec((1, m, R * HEAD), lambda b, g: (b, 0, g)),
        out_shape=jax.ShapeDtypeStruct((B, m, H * HEAD), BF16),
        compiler_params=_params("parallel", "parallel"),
        name="sink_attn",
    )(sink.astype(F32), qkv, qkv, qkv)


def _silu(g):
    return g * jax.nn.sigmoid(g)


def _ffn_kernel(x_ref, sh_ref, sc_ref, gate_ref, w1_ref, w3_ref, w2_ref, g_ref, b_ref, o_ref, h_ref, acc_ref,
                *, alpha, nf):
    f = pl.program_id(2)

    @pl.when(f == 0)
    def _():
        h_ref[...] = (x_ref[0] * (1.0 + sc_ref[0]) + sh_ref[0]).astype(BF16)
        acc_ref[...] = jnp.zeros(acc_ref.shape, F32)

    h = h_ref[...]
    a = _silu(jnp.dot(h, w1_ref[0], preferred_element_type=F32)) * jnp.dot(h, w3_ref[0], preferred_element_type=F32)
    acc_ref[...] += jnp.dot(a.astype(BF16), w2_ref[0], preferred_element_type=F32)

    @pl.when(f == nf - 1)
    def _():
        o_ref[0] = _res_ln(x_ref[0], acc_ref[...], gate_ref[0], g_ref[...], b_ref[...], alpha)


def _ffn(x, w13, w2, slot, mod, g, b, *, ctx, alpha):
    B, R, D = x.shape
    Fd = w2.shape[1]
    tm = _tile(R, TM)
    tf = _tile(Fd, TF, LANE)
    nf = Fd // tf
    row = _mod_row(ctx, B)
    vec = pl.BlockSpec((1, 1, D), lambda b, i, f: (row(b), 0, 0))
    return pl.pallas_call(
        functools.partial(_ffn_kernel, alpha=alpha, nf=nf),
        grid=(B, R // tm, nf),
        in_specs=[pl.BlockSpec((1, tm, D), lambda b, i, f: (b, i, 0)), vec, vec, vec,
                  pl.BlockSpec((1, D, tf), lambda b, i, f: (slot, 0, f)),
                  pl.BlockSpec((1, D, tf), lambda b, i, f: (slot, 0, nf + f)),
                  pl.BlockSpec((1, tf, D), lambda b, i, f: (slot, f, 0)),
                  pl.BlockSpec((1, D), lambda b, i, f: (0, 0)),
                  pl.BlockSpec((1, D), lambda b, i, f: (0, 0))],
        out_specs=pl.BlockSpec((1, tm, D), lambda b, i, f: (b, i, 0)),
        out_shape=jax.ShapeDtypeStruct((B, R, D), F32),
        scratch_shapes=[pltpu.VMEM((tm, D), BF16), pltpu.VMEM((tm, D), F32)],
        compiler_params=_params("parallel", "parallel", "arbitrary"),
        name="ffn",
    )(x, mod[0], mod[1], mod[2], w13, w13, w2, g.reshape(1, D), b.reshape(1, D))


def _router_kernel(x_ref, sh_ref, sc_ref, rt_ref, h_ref, idx_ref, wts_ref):
    hf = x_ref[0] * (1.0 + sc_ref[0]) + sh_ref[0]
    h_ref[0] = hf
    logits = jnp.dot(hf, rt_ref[...], preferred_element_type=F32, precision=lax.Precision.HIGHEST)
    E = logits.shape[1]
    lane = lax.broadcasted_iota(jnp.int32, logits.shape, 1)
    m1 = jnp.max(logits, axis=1, keepdims=True)
    i1 = jnp.min(jnp.where(logits == m1, lane, E), axis=1, keepdims=True)
    rest = jnp.where(lane == i1, -jnp.inf, logits)
    m2 = jnp.max(rest, axis=1, keepdims=True)
    i2 = jnp.min(jnp.where(rest == m2, lane, E), axis=1, keepdims=True)
    w1 = 1.0 / (1.0 + jnp.exp(m2 - m1))
    two = lax.broadcasted_iota(jnp.int32, idx_ref.shape[1:], 1)
    idx_ref[0] = jnp.where(two == 0, i1, i2)
    wts_ref[0] = jnp.where(two == 0, w1, 1.0 - w1)


def _router(x, router, mod, *, ctx):
    B, R, D = x.shape
    E = router.shape[1]
    tm = _tile(R, TM)
    row = _mod_row(ctx, B)
    vec = pl.BlockSpec((1, 1, D), lambda b, i: (row(b), 0, 0))
    return pl.pallas_call(
        _router_kernel,
        grid=(B, R // tm),
        in_specs=[pl.BlockSpec((1, tm, D), lambda b, i: (b, i, 0)), vec, vec,
                  pl.BlockSpec((D, E), lambda b, i: (0, 0))],
        out_specs=[pl.BlockSpec((1, tm, D), lambda b, i: (b, i, 0)),
                   pl.BlockSpec((1, tm, 2), lambda b, i: (b, i, 0)),
                   pl.BlockSpec((1, tm, 2), lambda b, i: (b, i, 0))],
        out_shape=[jax.ShapeDtypeStruct((B, R, D), F32),
                   jax.ShapeDtypeStruct((B, R, 2), jnp.int32),
                   jax.ShapeDtypeStruct((B, R, 2), F32)],
        compiler_params=_params("parallel", "parallel"),
        name="router",
    )(x, mod[0], mod[1], router)


def _route_plan(idx, wts, E, tm):
    T = idx.shape[0]
    P = 2 * T
    e_flat = idx.reshape(P)
    onehot = (e_flat[:, None] == jnp.arange(E, dtype=jnp.int32)[None, :]).astype(jnp.int32)
    counts = jnp.sum(onehot, axis=0)
    rank = jnp.take_along_axis(jnp.cumsum(onehot, axis=0) - onehot, e_flat[:, None], axis=1)[:, 0]
    padded = ((counts + tm - 1) // tm) * tm
    ends = jnp.cumsum(padded)
    starts = ends - padded
    slot = starts[e_flat] + rank
    ntiles = P // tm + E
    S = ntiles * tm
    tile_start = jnp.arange(ntiles, dtype=jnp.int32) * tm
    tile_expert = jnp.minimum(jnp.searchsorted(ends, tile_start, side="right"), E - 1).astype(jnp.int32)
    tile_rows = jnp.clip(counts[tile_expert] - (tile_start - starts[tile_expert]), 0, tm)
    tile_rows = jnp.where(tile_start < ends[E - 1], tile_rows, 0).astype(jnp.int32)
    last = jnp.maximum(jnp.sum((tile_rows > 0).astype(jnp.int32)) - 1, 0)
    tile_expert = jnp.where(tile_rows > 0, tile_expert, tile_expert[last])
    pair = jnp.zeros((S,), jnp.int32).at[slot].set(jnp.arange(P, dtype=jnp.int32))
    token, choice = pair // 2, pair % 2
    src = token
    dst = choice * T + token
    gate = wts.reshape(P)[pair]
    return tile_expert, tile_rows, src.reshape(ntiles, 1, tm), dst.reshape(ntiles, 1, tm), gate.reshape(S, 1)


def _experts_kernel(te_ref, tr_ref, src_ref, srcn_ref, dst_ref, gate_ref, h_hbm, w1_ref, w3_ref, w2_ref, y_hbm,
                    rows_ref, hb_ref, acc_ref, sem_in, sem_out, *, nf, tm, ntiles, gsteps):
    t = pl.program_id(0)
    f = pl.program_id(1)
    nrows = tr_ref[t]
    valid = nrows > 0
    slot = t % 2
    chunk = tm // gsteps

    def gather_row(idx_ref, r, s):
        return pltpu.make_async_copy(h_hbm.at[pl.ds(idx_ref[0, 0, r], 1)], rows_ref.at[s, pl.ds(r, 1)], sem_in)

    def scatter_wait(n, s):
        nfull = pl.multiple_of((n // SUBLANE) * SUBLANE, SUBLANE)

        @pl.when(nfull > 0)
        def _():
            pltpu.make_async_copy(rows_ref.at[s, pl.ds(0, nfull)], y_hbm.at[pl.ds(0, nfull)], sem_out).wait()

        def single(r, carry):
            pltpu.make_async_copy(rows_ref.at[s, pl.ds(r, 1)], y_hbm.at[pl.ds(0, 1)], sem_out).wait()
            return carry

        lax.fori_loop(nfull, n, single, 0)

    @pl.when((t == 0) & (f == 0))
    def _():
        def group(i, carry):
            for u in range(ROW_UNROLL):
                gather_row(src_ref, i * ROW_UNROLL + u, 0).start()
            return carry

        lax.fori_loop(0, tm // ROW_UNROLL, group, 0)

    @pl.when(f == 0)
    def _():
        pltpu.make_async_copy(h_hbm.at[pl.ds(0, tm)], rows_ref.at[slot], sem_in).wait()

    @pl.when(valid & (f == 0))
    def _():
        hb_ref[...] = rows_ref[slot].astype(BF16)
        acc_ref[...] = jnp.zeros(acc_ref.shape, F32)

    @pl.when((f == 1) & (t >= 1))
    def _():
        scatter_wait(tr_ref[jnp.maximum(t - 1, 0)], 1 - slot)

    def matmuls():
        h = hb_ref[...]
        a = _silu(jnp.dot(h, w1_ref[0], preferred_element_type=F32)) * jnp.dot(h, w3_ref[0], preferred_element_type=F32)
        acc_ref[...] += jnp.dot(a.astype(BF16), w2_ref[0], preferred_element_type=F32)

    def prefetch_chunk():
        base = (f - 1) * chunk
        for u in range(chunk):
            gather_row(srcn_ref, base + u, 1 - slot).start()

    prefetch = (t + 1 < ntiles) & (f >= 1) & (f <= gsteps)

    @pl.when(valid & prefetch)
    def _():
        matmuls()
        prefetch_chunk()

    @pl.when(valid & jnp.logical_not(prefetch))
    def _():
        matmuls()

    @pl.when(jnp.logical_not(valid) & prefetch)
    def _():
        prefetch_chunk()

    @pl.when(valid & (f == nf - 1))
    def _():
        rows_ref[slot] = acc_ref[...] * gate_ref[...]

        def scatter_row(r):
            return pltpu.make_async_copy(rows_ref.at[slot, pl.ds(r, 1)],
                                         y_hbm.at[pl.ds(dst_ref[0, 0, r], 1)], sem_out)

        ngroups = nrows // ROW_UNROLL

        def group(i, carry):
            for u in range(ROW_UNROLL):
                scatter_row(i * ROW_UNROLL + u).start()
            return carry

        def single(r, carry):
            scatter_row(r).start()
            return carry

        lax.fori_loop(0, ngroups, group, 0)
        lax.fori_loop(ngroups * ROW_UNROLL, nrows, single, 0)

    @pl.when(valid & (f == nf - 1) & (t == ntiles - 1))
    def _():
        scatter_wait(nrows, slot)


def _experts(h, plan, w13, w2, e_off, tm):
    T, D = h.shape
    Fd = w2.shape[1]
    tile_expert, tile_rows, src, dst, gate = plan
    tile_expert = tile_expert + e_off
    ntiles = src.shape[0]
    tf = _tile(Fd, TF, LANE)
    nf = Fd // tf
    assert nf >= 2, "the gather of the next tile is spread over the steps after the first"
    gsteps = 1
    while gsteps * 2 <= nf - 1 and tm % (gsteps * 2) == 0:
        gsteps *= 2
    fidx = lambda t, f, tv: jnp.where(tv[t] > 0, f, nf - 1)
    grid_spec = pltpu.PrefetchScalarGridSpec(
        num_scalar_prefetch=2,
        grid=(ntiles, nf),
        in_specs=[pl.BlockSpec((1, 1, tm), lambda t, f, te, tv: (t, 0, 0), memory_space=pltpu.SMEM),
                  pl.BlockSpec((1, 1, tm), lambda t, f, te, tv: (jnp.minimum(t + 1, ntiles - 1), 0, 0),
                               memory_space=pltpu.SMEM),
                  pl.BlockSpec((1, 1, tm), lambda t, f, te, tv: (t, 0, 0), memory_space=pltpu.SMEM),
                  pl.BlockSpec((tm, 1), lambda t, f, te, tv: (t, 0)),
                  pl.BlockSpec(memory_space=pl.ANY),
                  pl.BlockSpec((1, D, tf), lambda t, f, te, tv: (te[t], 0, fidx(t, f, tv))),
                  pl.BlockSpec((1, D, tf), lambda t, f, te, tv: (te[t], 0, nf + fidx(t, f, tv))),
                  pl.BlockSpec((1, tf, D), lambda t, f, te, tv: (te[t], fidx(t, f, tv), 0))],
        out_specs=pl.BlockSpec(memory_space=pl.ANY),
        scratch_shapes=[pltpu.VMEM((2, tm, D), F32), pltpu.VMEM((tm, D), BF16), pltpu.VMEM((tm, D), F32),
                        pltpu.SemaphoreType.DMA(()), pltpu.SemaphoreType.DMA(())],
    )
    return pl.pallas_call(
        functools.partial(_experts_kernel, nf=nf, tm=tm, ntiles=ntiles, gsteps=gsteps),
        grid_spec=grid_spec,
        out_shape=jax.ShapeDtypeStruct((2 * T, D), F32),
        compiler_params=pltpu.CompilerParams(dimension_semantics=("arbitrary", "arbitrary"),
                                             vmem_limit_bytes=VMEM_LIMIT),
        name="experts",
    )(tile_expert, tile_rows, src, src, dst, gate, h, w13, w13, w2)


def _combine_kernel(x_ref, y0_ref, y1_ref, gate_ref, g_ref, b_ref, o_ref, *, alpha):
    o_ref[0] = _res_ln(x_ref[0], y0_ref[...] + y1_ref[...], gate_ref[0], g_ref[...], b_ref[...], alpha)


def _combine_ln(x, y, gate, g, b, *, ctx, alpha):
    B, R, D = x.shape
    tm = _tile(R, TM)
    nb = R // tm
    row = _mod_row(ctx, B)
    return pl.pallas_call(
        functools.partial(_combine_kernel, alpha=alpha),
        grid=(B, nb),
        in_specs=[pl.BlockSpec((1, tm, D), lambda b, i: (b, i, 0)),
                  pl.BlockSpec((tm, D), lambda b, i: (b * nb + i, 0)),
                  pl.BlockSpec((tm, D), lambda b, i: (B * nb + b * nb + i, 0)),
                  pl.BlockSpec((1, 1, D), lambda b, i: (row(b), 0, 0)),
                  pl.BlockSpec((1, D), lambda b, i: (0, 0)),
                  pl.BlockSpec((1, D), lambda b, i: (0, 0))],
        out_specs=pl.BlockSpec((1, tm, D), lambda b, i: (b, i, 0)),
        out_shape=jax.ShapeDtypeStruct((B, R, D), F32),
        compiler_params=_params("parallel", "parallel"),
        name="combine_ln",
    )(x, y, y, gate, g.reshape(1, D), b.reshape(1, D))


def _moe(x, router, w13, w2, slot, mod, g, b, *, ctx, alpha):
    B, R, D = x.shape
    E = router.shape[1]
    T = B * R
    tm = _tile(2 * T, TME)
    h, idx, wts = _router(x, router, mod, ctx=ctx)
    plan = _route_plan(idx.reshape(T, 2), wts.reshape(T, 2), E, tm)
    y = _experts(h.reshape(T, D), plan, w13, w2, slot * E, tm)
    return _combine_ln(x, y, mod[2], g, b, ctx=ctx, alpha=alpha)


def _mix_diff(xl, xc, mod, need_ctx, layer, wqkv, lam, subln_g, rope128):
    D = xl.shape[2]
    H = D // (2 * HEAD)
    lam_init = 0.8 - 0.6 * math.exp(-0.3 * layer)
    w = wqkv.astype(BF16)
    qs = HEAD ** -0.5 * LOG2E
    qkv_l = _proj(xl, w, mod=mod, ctx=False, rope=rope128, rope_cols=(0, 2 * D), q_cols=D, q_scale=qs)
    qkv_c = _proj(xc, w, mod=mod, ctx=True, q_cols=D, q_scale=qs)
    kv_all = jnp.concatenate([qkv_l, qkv_c], axis=1)
    o_l = _diff_attn(qkv_l, kv_all, lam, subln_g, lam_init, H)
    o_c = _diff_attn(qkv_c, qkv_c, lam, subln_g, lam_init, H) if need_ctx else None
    return o_l, o_c


def _mix_window(xl, xc, mod, need_ctx, wqkv, sink, rope128):
    D = xl.shape[2]
    H = D // HEAD
    G = H // 4
    w = wqkv.astype(BF16)
    qkv_l = _proj(xl, w, mod=mod, ctx=False, rope=rope128, rope_cols=(0, (H + G) * HEAD))
    qkv_c = _proj(xc, w, mod=mod, ctx=True)
    o_l = _win_attn(qkv_l, qkv_c, sink, H, G)
    o_c = _sink_attn(qkv_c, sink, H, G) if need_ctx else None
    return o_l, o_c


def _mix_mla(xl, xc, mod, need_ctx, w_down, gq, gkv, w_uq, w_ukv, rope64):
    D = xl.shape[2]
    H = D // HEAD
    rq, rkv = gq.shape[0], gkv.shape[0]
    wd = jnp.pad(w_down, ((0, 0), (0, LANE - MLA_ROPE))).astype(BF16)
    wq3 = w_uq.reshape(rq, H, HEAD + MLA_ROPE)
    wq = jnp.concatenate(
        [wq3[:, :, :HEAD].reshape(rq, H * HEAD),
         jnp.pad(wq3[:, :, HEAD:], ((0, 0), (0, 0), (0, HEAD - MLA_ROPE))).reshape(rq, H * HEAD)], axis=1).astype(BF16)
    wkv3 = w_ukv.reshape(rkv, H, 2 * HEAD)
    wkv = jnp.concatenate([wkv3[:, :, :HEAD].reshape(rkv, H * HEAD),
                           wkv3[:, :, HEAD:].reshape(rkv, H * HEAD)], axis=1).astype(BF16)
    cq_l, ckv_l, kr_l = _mla_down(xl, wd, gq, gkv, mod, ctx=False, rope=rope64)
    cq_c, ckv_c, kr_c = _mla_down(xc, wd, gq, gkv, mod, ctx=True, rope=None)
    qs = (HEAD + MLA_ROPE) ** -0.5 * LOG2E
    q_l = _proj(cq_l, wq, rope=rope64, rope_cols=(H * HEAD, 2 * H * HEAD), q_cols=2 * H * HEAD, q_scale=qs)
    kv_l = _proj(ckv_l, wkv)
    kv_c = _proj(ckv_c, wkv)
    kv_all = jnp.concatenate([kv_l, kv_c], axis=1)
    kr_all = jnp.concatenate([kr_l, kr_c], axis=1)
    o_l = _mla_attn(q_l, kv_all, kr_all, H)
    o_c = None
    if need_ctx:
        q_c = _proj(cq_c, wq, q_cols=2 * H * HEAD, q_scale=qs)
        o_c = _mla_attn(q_c, kv_c, kr_c, H)
    return o_l, o_c


def kernel(x, c, ctx, c_ctx, mod_w, mod_b, ln1_g, ln1_b, ln2_g, ln2_b, da_wqkv, da_lambda, da_subln_g, da_wo, wa_wqkv, wa_sink, wa_wo, mla_w_down, mla_q_norm_g, mla_kv_norm_g, mla_w_uq, mla_w_ukv, mla_wo, ffn_w13, ffn_w2, moe_router, moe_w13, moe_w2):
    B, n, D = x.shape
    depth = mod_w.shape[0]
    alpha = (2.0 * depth) ** 0.25
    rows = SUBLANE * (-(-(B + 1) // SUBLANE))
    cc = jnp.concatenate([c, c_ctx[None, :], jnp.zeros((rows - B - 1, D), F32)], axis=0)
    mod_all = _modulation(cc, mod_w, mod_b)
    rope128 = _rope_tables(n, HEAD)
    rope64 = _rope_tables(n, MLA_ROPE)
    ffn_w13_b, ffn_w2_b = ffn_w13.astype(BF16), ffn_w2.astype(BF16)
    moe_w13_b = moe_w13.astype(BF16).reshape((-1,) + moe_w13.shape[2:])
    moe_w2_b = moe_w2.astype(BF16).reshape((-1,) + moe_w2.shape[2:])
    xl, xc = x, ctx
    for i in range(depth):
        need_ctx = i < depth - 1
        mod = [mod_all[i, :, k * D:(k + 1) * D].reshape(rows, 1, D) for k in range(6)]
        kind, slot = i % N_MIXERS, i // N_MIXERS
        if kind == 0:
            o_l, o_c = _mix_diff(xl, xc, mod[0:2], need_ctx, i, da_wqkv[slot], da_lambda[slot],
                                 da_subln_g[slot], rope128)
            wo = da_wo[slot]
        elif kind == 1:
            o_l, o_c = _mix_window(xl, xc, mod[0:2], need_ctx, wa_wqkv[slot], wa_sink[slot], rope128)
            wo = wa_wo[slot]
        else:
            o_l, o_c = _mix_mla(xl, xc, mod[0:2], need_ctx, mla_w_down[slot], mla_q_norm_g[slot],
                                mla_kv_norm_g[slot], mla_w_uq[slot], mla_w_ukv[slot], rope64)
            wo = mla_wo[slot]
        wo = wo.astype(BF16)
        xl = _oproj_ln(xl, o_l, wo, mod[2], ln1_g[i], ln1_b[i], ctx=False, alpha=alpha)
        if need_ctx:
            xc = _oproj_ln(xc, o_c, wo, mod[2], ln1_g[i], ln1_b[i], ctx=True, alpha=alpha)
        cslot = i // 2
        if i % 2 == 0:
            xl = _ffn(xl, ffn_w13_b, ffn_w2_b, cslot, mod[3:6], ln2_g[i], ln2_b[i], ctx=False, alpha=alpha)
            if need_ctx:
                xc = _ffn(xc, ffn_w13_b, ffn_w2_b, cslot, mod[3:6], ln2_g[i], ln2_b[i], ctx=True, alpha=alpha)
        else:
            xl = _moe(xl, moe_router[cslot], moe_w13_b, moe_w2_b, cslot, mod[3:6], ln2_g[i], ln2_b[i],
                      ctx=False, alpha=alpha)
            if need_ctx:
                xc = _moe(xc, moe_router[cslot], moe_w13_b, moe_w2_b, cslot, mod[3:6], ln2_g[i], ln2_b[i],
                          ctx=True, alpha=alpha)
    return xl
```

```python
import functools
import math

import jax
import jax.numpy as jnp
from jax import lax
from jax.experimental import pallas as pl
from jax.experimental.pallas import tpu as pltpu

F32 = jnp.float32
BF16 = jnp.bfloat16

LANE = 128
SUBLANE = 8
VMEM_LIMIT = 56 * 1024 * 1024

GRID_W = 64
ROPE_BASE = 10000.0
LN_EPS = 1e-5
RMS_EPS = 1e-6
NEG_INF = -1e30
HEAD = 128
WINDOW = 128
MLA_ROPE = 64
N_MIXERS = 3
LOG2E = math.log2(math.e)

TM = 512
TMP = 1024
TW = 512
TN = 1024
TF = 512
TQ = 2048
TK = 2816
RB = 256
TME = 1024
ROW_UNROLL = 8


def _tile(dim, pref, align=SUBLANE):
    if dim <= pref:
        return dim
    t = (pref // align) * align
    while t >= align:
        if dim % t == 0:
            return t
        t -= align
    return dim


def _params(*sem):
    return pltpu.CompilerParams(dimension_semantics=sem, vmem_limit_bytes=VMEM_LIMIT)


def _mod_row(ctx, nb):
    return (lambda b: nb) if ctx else (lambda b: b)


def _mod_kernel(s_ref, w_ref, b_ref, o_ref):
    s = s_ref[...]
    s = s * jax.nn.sigmoid(s)
    o_ref[0] = jnp.dot(s.astype(BF16), w_ref[0].astype(BF16), preferred_element_type=F32) + b_ref[0]


def _modulation(cc, mod_w, mod_b):
    depth, D, N = mod_w.shape
    R = cc.shape[0]
    tn = _tile(N, 1024, LANE)
    return pl.pallas_call(
        _mod_kernel,
        grid=(depth, N // tn),
        in_specs=[pl.BlockSpec((R, D), lambda l, j: (0, 0)),
                  pl.BlockSpec((1, D, tn), lambda l, j: (l, 0, j)),
                  pl.BlockSpec((1, 1, tn), lambda l, j: (l, 0, j))],
        out_specs=pl.BlockSpec((1, R, tn), lambda l, j: (l, 0, j)),
        out_shape=jax.ShapeDtypeStruct((depth, R, N), F32),
        compiler_params=_params("parallel", "parallel"),
        name="modulation",
    )(cc, mod_w, mod_b.reshape(depth, 1, N))


def _rope_tables(n, dim):
    half = dim // 4
    t = jnp.arange(n)
    row = (t // GRID_W).astype(F32)
    col = (t % GRID_W).astype(F32)
    inv = ROPE_BASE ** (-jnp.arange(half, dtype=F32) / half)
    ar = row[:, None] * inv[None, :]
    ac = col[:, None] * inv[None, :]
    cos = jnp.concatenate([jnp.cos(ar), jnp.cos(ar), jnp.cos(ac), jnp.cos(ac)], axis=1)
    sin = jnp.concatenate([-jnp.sin(ar), jnp.sin(ar), -jnp.sin(ac), jnp.sin(ac)], axis=1)
    rep = LANE // dim
    return jnp.tile(cos, (1, rep)), jnp.tile(sin, (1, rep)), half


def _rope_chunk(yc, cos, sin, half):
    lane = lax.broadcasted_iota(jnp.int32, (1, LANE), 1)
    first = (lane % (2 * half)) < half
    rot = jnp.where(first, pltpu.roll(yc, LANE - half, 1), pltpu.roll(yc, half, 1))
    return yc * cos + rot * sin


def _proj_kernel(*refs, modulate, rope_lo, rope_hi, half, q_hi, q_scale):
    refs = list(refs)
    x_ref = refs.pop(0)
    if modulate:
        sh_ref, sc_ref = refs.pop(0), refs.pop(0)
    w_ref = refs.pop(0)
    if rope_hi > rope_lo:
        cos_ref, sin_ref = refs.pop(0), refs.pop(0)
    o_ref = refs.pop(0)
    j = pl.program_id(2)
    if modulate:
        h_ref = refs.pop(0)

        @pl.when(j == 0)
        def _():
            h_ref[...] = (x_ref[0] * (1.0 + sc_ref[0]) + sh_ref[0]).astype(BF16)

        h = h_ref[...]
    else:
        h = x_ref[0]
    y = jnp.dot(h, w_ref[...], preferred_element_type=F32)
    if q_hi > 0:
        y = y * jnp.where(j < q_hi, q_scale, 1.0)
    if rope_hi > rope_lo:
        in_rope = (j >= rope_lo) & (j < rope_hi)

        @pl.when(in_rope)
        def _():
            cos, sin = cos_ref[...], sin_ref[...]
            for cidx in range(y.shape[1] // LANE):
                sl = slice(cidx * LANE, (cidx + 1) * LANE)
                o_ref[0, :, sl] = _rope_chunk(y[:, sl], cos, sin, half).astype(o_ref.dtype)

        @pl.when(jnp.logical_not(in_rope))
        def _():
            o_ref[0] = y.astype(o_ref.dtype)
    else:
        o_ref[0] = y.astype(o_ref.dtype)


def _proj(x, w, *, mod=None, ctx=False, rope=None, rope_cols=(0, 0), q_cols=0, q_scale=1.0):
    B, R, K = x.shape
    N = w.shape[1]
    tm = _tile(R, TMP)
    g = math.gcd(N, q_cols)
    if rope is not None:
        g = math.gcd(g, math.gcd(rope_cols[0], rope_cols[1]))
    tn = _tile(g, TN, LANE)
    modulate = mod is not None
    args = [x]
    in_specs = [pl.BlockSpec((1, tm, K), lambda b, i, j: (b, i, 0))]
    if modulate:
        shift, scale = mod
        row = _mod_row(ctx, B)
        in_specs += [pl.BlockSpec((1, 1, K), lambda b, i, j: (row(b), 0, 0))] * 2
        args += [shift, scale]
    in_specs.append(pl.BlockSpec((K, tn), lambda b, i, j: (0, j)))
    args.append(w)
    half = 0
    if rope is not None:
        cos, sin, half = rope
        in_specs += [pl.BlockSpec((tm, LANE), lambda b, i, j: (i, 0))] * 2
        args += [cos, sin]
    kern = functools.partial(_proj_kernel, modulate=modulate, rope_lo=rope_cols[0] // tn,
                             rope_hi=rope_cols[1] // tn if rope is not None else 0, half=half,
                             q_hi=q_cols // tn, q_scale=q_scale)
    return pl.pallas_call(
        kern,
        grid=(B, R // tm, N // tn),
        in_specs=in_specs,
        out_specs=pl.BlockSpec((1, tm, tn), lambda b, i, j: (b, i, j)),
        out_shape=jax.ShapeDtypeStruct((B, R, N), BF16),
        scratch_shapes=[pltpu.VMEM((tm, K), BF16)] if modulate else [],
        compiler_params=_params("parallel", "parallel", "arbitrary"),
        name="proj",
    )(*args)


def _res_ln(x, y, gate, g, b, alpha):
    z = alpha * x + gate * y
    mu = jnp.mean(z, axis=-1, keepdims=True)
    zc = z - mu
    var = jnp.mean(zc * zc, axis=-1, keepdims=True)
    return zc * lax.rsqrt(var + LN_EPS) * g + b


def _oproj_kernel(x_ref, o_ref, w_ref, gate_ref, g_ref, b_ref, out_ref, *, alpha):
    y = jnp.dot(o_ref[0], w_ref[...], preferred_element_type=F32)
    out_ref[0] = _res_ln(x_ref[0], y, gate_ref[0], g_ref[...], b_ref[...], alpha)


def _oproj_ln(x, o, w, gate, g, b, *, ctx, alpha):
    B, R, D = x.shape
    K = o.shape[2]
    tm = _tile(R, TM)
    row = _mod_row(ctx, B)
    return pl.pallas_call(
        functools.partial(_oproj_kernel, alpha=alpha),
        grid=(B, R // tm),
        in_specs=[pl.BlockSpec((1, tm, D), lambda b, i: (b, i, 0)),
                  pl.BlockSpec((1, tm, K), lambda b, i: (b, i, 0)),
                  pl.BlockSpec((K, D), lambda b, i: (0, 0)),
                  pl.BlockSpec((1, 1, D), lambda b, i: (row(b), 0, 0)),
                  pl.BlockSpec((1, D), lambda b, i: (0, 0)),
                  pl.BlockSpec((1, D), lambda b, i: (0, 0))],
        out_specs=pl.BlockSpec((1, tm, D), lambda b, i: (b, i, 0)),
        out_shape=jax.ShapeDtypeStruct((B, R, D), F32),
        compiler_params=_params("parallel", "parallel"),
        name="oproj_ln",
    )(x, o, w, gate, g.reshape(1, D), b.reshape(1, D))


def _flash_init(m_ref, l_ref, acc_ref):
    m_ref[...] = jnp.full(m_ref.shape, NEG_INF, F32)
    l_ref[...] = jnp.zeros(l_ref.shape, F32)
    acc_ref[...] = jnp.zeros(acc_ref.shape, F32)


def _dot_nt(a, b):
    return lax.dot_general(a, b, (((1,), (1,)), ((), ())), preferred_element_type=F32)


def _flash_step(q_fn, k_fn, v_fn, m_ref, l_ref, acc_ref, *, chains, tq):
    rb = min(RB, tq)

    def body(r, carry):
        rows = pl.ds(pl.multiple_of(r * rb, rb), rb)
        for c in range(chains):
            s = _dot_nt(q_fn(c, rows), k_fn(c))
            m_prev = m_ref[c, rows]
            m_new = jnp.maximum(m_prev, jnp.max(s, axis=1, keepdims=True))
            a = jnp.exp2(m_prev - m_new)
            p = jnp.exp2(s - m_new)
            l_ref[c, rows] = a * l_ref[c, rows] + jnp.sum(p, axis=1, keepdims=True)
            acc_ref[c, rows] = a * acc_ref[c, rows] + jnp.dot(p.astype(BF16), v_fn(c), preferred_element_type=F32)
            m_ref[c, rows] = m_new
        return carry

    lax.fori_loop(0, tq // rb, body, 0, unroll=True)


def _diff_attn_kernel(lam_ref, g_ref, q_ref, k_ref, v_ref, o_ref, m_ref, l_ref, acc_ref, *, lam_init, nk):
    ki = pl.program_id(3)

    @pl.when(ki == 0)
    def _():
        _flash_init(m_ref, l_ref, acc_ref)

    _flash_step(lambda c, rows: q_ref[0, rows, c * HEAD:(c + 1) * HEAD],
                lambda c: k_ref[0, :, c * HEAD:(c + 1) * HEAD],
                lambda c: v_ref[0],
                m_ref, l_ref, acc_ref, chains=2, tq=q_ref.shape[1])

    @pl.when(ki == nk - 1)
    def _():
        lam = lam_ref[...]
        lam_full = (jnp.exp(jnp.sum(lam[0:1] * lam[1:2], axis=1, keepdims=True))
                    - jnp.exp(jnp.sum(lam[2:3] * lam[3:4], axis=1, keepdims=True)) + lam_init)
        o = acc_ref[0] / l_ref[0] - lam_full * (acc_ref[1] / l_ref[1])
        ms = jnp.mean(o * o, axis=1, keepdims=True)
        o_ref[0] = (o * lax.rsqrt(ms + RMS_EPS) * g_ref[...] * (1.0 - lam_init)).astype(o_ref.dtype)


def _diff_attn(q_src, kv_src, lam, subln_g, lam_init, H):
    B, nq, _ = q_src.shape
    nkv = kv_src.shape[1]
    W = 2 * HEAD
    tq = _tile(nq, TQ)
    tk = _tile(nkv, TK)
    nk = nkv // tk
    return pl.pallas_call(
        functools.partial(_diff_attn_kernel, lam_init=lam_init, nk=nk),
        grid=(B, H, nq // tq, nk),
        in_specs=[pl.BlockSpec((4, HEAD), lambda b, h, i, k: (0, 0)),
                  pl.BlockSpec((1, W), lambda b, h, i, k: (0, 0)),
                  pl.BlockSpec((1, tq, W), lambda b, h, i, k: (b, i, h)),
                  pl.BlockSpec((1, tk, W), lambda b, h, i, k: (b, k, H + h)),
                  pl.BlockSpec((1, tk, W), lambda b, h, i, k: (b, k, 2 * H + h))],
        out_specs=pl.BlockSpec((1, tq, W), lambda b, h, i, k: (b, i, h)),
        out_shape=jax.ShapeDtypeStruct((B, nq, H * W), BF16),
        scratch_shapes=[pltpu.VMEM((2, tq, 1), F32), pltpu.VMEM((2, tq, 1), F32),
                        pltpu.VMEM((2, tq, W), F32)],
        compiler_params=_params("parallel", "parallel", "parallel", "arbitrary"),
        name="diff_attn",
    )(lam.astype(F32), subln_g.reshape(1, W), q_src, kv_src, kv_src)


def _mla_attn_kernel(qn_ref, qr_ref, kn_ref, kr_ref, v_ref, o_ref, m_ref, l_ref, acc_ref, k_ref, *, nk):
    ki = pl.program_id(3)

    @pl.when(ki == 0)
    def _():
        _flash_init(m_ref, l_ref, acc_ref)

    for c in range(2):
        k_ref[c, :, :HEAD] = kn_ref[0, :, c * HEAD:(c + 1) * HEAD]
        k_ref[c, :, HEAD:] = kr_ref[0]

    def q_fn(c, rows):
        sl = slice(c * HEAD, (c + 1) * HEAD)
        return jnp.concatenate([qn_ref[0, rows, sl], qr_ref[0, rows, sl]], axis=1)

    _flash_step(q_fn, lambda c: k_ref[c], lambda c: v_ref[0, :, c * HEAD:(c + 1) * HEAD],
                m_ref, l_ref, acc_ref, chains=2, tq=qn_ref.shape[1])

    @pl.when(ki == nk - 1)
    def _():
        for c in range(2):
            o_ref[0, :, c * HEAD:(c + 1) * HEAD] = (acc_ref[c] / l_ref[c]).astype(o_ref.dtype)


def _mla_attn(q_src, kv_src, kr_src, H):
    B, nq, _ = q_src.shape
    nkv = kv_src.shape[1]
    assert H % 2 == 0
    P = H // 2
    W = 2 * HEAD
    tq = _tile(nq, TQ)
    tk = _tile(nkv, TK)
    nk = nkv // tk
    return pl.pallas_call(
        functools.partial(_mla_attn_kernel, nk=nk),
        grid=(B, P, nq // tq, nk),
        in_specs=[pl.BlockSpec((1, tq, W), lambda b, h, i, k: (b, i, h)),
                  pl.BlockSpec((1, tq, W), lambda b, h, i, k: (b, i, P + h)),
                  pl.BlockSpec((1, tk, W), lambda b, h, i, k: (b, k, h)),
                  pl.BlockSpec((1, tk, HEAD), lambda b, h, i, k: (b, k, 0)),
                  pl.BlockSpec((1, tk, W), lambda b, h, i, k: (b, k, P + h))],
        out_specs=pl.BlockSpec((1, tq, W), lambda b, h, i, k: (b, i, h)),
        out_shape=jax.ShapeDtypeStruct((B, nq, H * HEAD), BF16),
        scratch_shapes=[pltpu.VMEM((2, tq, 1), F32), pltpu.VMEM((2, tq, 1), F32),
                        pltpu.VMEM((2, tq, HEAD), F32), pltpu.VMEM((2, tk, W), BF16)],
        compiler_params=_params("parallel", "parallel", "parallel", "arbitrary"),
        name="mla_attn",
    )(q_src, q_src, kv_src, kr_src, kv_src)


def _mla_down_kernel(x_ref, sh_ref, sc_ref, w_ref, gq_ref, gkv_ref, *rest, rq, rkv, rope):
    if rope:
        cos_ref, sin_ref, cq_ref, ckv_ref, kr_ref = rest
    else:
        cq_ref, ckv_ref, kr_ref = rest
    h = (x_ref[0] * (1.0 + sc_ref[0]) + sh_ref[0]).astype(BF16)
    y = jnp.dot(h, w_ref[...], preferred_element_type=F32)

    def rms(t, g):
        return t * lax.rsqrt(jnp.mean(t * t, axis=1, keepdims=True) + RMS_EPS) * g

    cq_ref[0] = rms(y[:, :rq], gq_ref[...]).astype(BF16)
    ckv_ref[0] = rms(y[:, rq:rq + rkv], gkv_ref[...]).astype(BF16)
    kr = y[:, rq + rkv:]
    if rope:
        kr = _rope_chunk(kr, cos_ref[...], sin_ref[...], MLA_ROPE // 4)
    kr_ref[0] = kr.astype(BF16)


def _mla_down(x, w, gq, gkv, mod, *, ctx, rope):
    B, R, D = x.shape
    rq, rkv = gq.shape[0], gkv.shape[0]
    N = w.shape[1]
    tm = _tile(R, TM)
    row = _mod_row(ctx, B)
    in_specs = [pl.BlockSpec((1, tm, D), lambda b, i: (b, i, 0)),
                pl.BlockSpec((1, 1, D), lambda b, i: (row(b), 0, 0)),
                pl.BlockSpec((1, 1, D), lambda b, i: (row(b), 0, 0)),
                pl.BlockSpec((D, N), lambda b, i: (0, 0)),
                pl.BlockSpec((1, rq), lambda b, i: (0, 0)),
                pl.BlockSpec((1, rkv), lambda b, i: (0, 0))]
    args = [x, mod[0], mod[1], w, gq.reshape(1, rq), gkv.reshape(1, rkv)]
    if rope is not None:
        in_specs += [pl.BlockSpec((tm, LANE), lambda b, i: (i, 0))] * 2
        args += [rope[0], rope[1]]
    return pl.pallas_call(
        functools.partial(_mla_down_kernel, rq=rq, rkv=rkv, rope=rope is not None),
        grid=(B, R // tm),
        in_specs=in_specs,
        out_specs=[pl.BlockSpec((1, tm, rq), lambda b, i: (b, i, 0)),
                   pl.BlockSpec((1, tm, rkv), lambda b, i: (b, i, 0)),
                   pl.BlockSpec((1, tm, LANE), lambda b, i: (b, i, 0))],
        out_shape=[jax.ShapeDtypeStruct((B, R, rq), BF16),
                   jax.ShapeDtypeStruct((B, R, rkv), BF16),
                   jax.ShapeDtypeStruct((B, R, LANE), BF16)],
        compiler_params=_params("parallel", "parallel"),
        name="mla_down",
    )(*args)


def _sink_softmax_pv(s, sk, v):
    mx = jnp.maximum(jnp.max(s, axis=1, keepdims=True), sk)
    e = jnp.exp(s - mx)
    r = 1.0 / (jnp.sum(e, axis=1, keepdims=True) + jnp.exp(sk - mx))
    return jnp.dot((e * r).astype(BF16), v, preferred_element_type=F32)


def _win_attn_kernel(sink_ref, q_ref, kp_ref, kc_ref, kn_ref, kx_ref, vp_ref, vc_ref, vn_ref, vx_ref, o_ref,
                     *, tq, n, R):
    g = pl.program_id(1)
    qi = pl.program_id(2)
    nsub = tq // WINDOW
    L = 3 * WINDOW + kx_ref.shape[1]
    row = lax.broadcasted_iota(jnp.int32, (R * WINDOW, L), 0)
    col = lax.broadcasted_iota(jnp.int32, (R * WINDOW, L), 1)
    head = lax.broadcasted_iota(jnp.int32, (R * WINDOW, 1), 0) // WINDOW
    pos = row % WINDOW
    sk = jnp.zeros((R * WINDOW, 1), F32)
    for r in range(R):
        sk = jnp.where(head == r, sink_ref[g * R + r], sk)
    scale = HEAD ** -0.5

    def piece(p_ref, c_ref, n_ref, j):
        if j < 0:
            return p_ref[0]
        if j >= nsub:
            return n_ref[0]
        return c_ref[0, j * WINDOW:(j + 1) * WINDOW]

    for j in range(nsub):
        k = jnp.concatenate([piece(kp_ref, kc_ref, kn_ref, jj) for jj in (j - 1, j, j + 1)] + [kx_ref[0]], axis=0)
        v = jnp.concatenate([piece(vp_ref, vc_ref, vn_ref, jj) for jj in (j - 1, j, j + 1)] + [vx_ref[0]], axis=0)
        base = qi * tq + j * WINDOW
        key_pos = base - WINDOW + col
        valid = ((jnp.abs(base + pos - key_pos) <= WINDOW) & (key_pos >= 0) & (key_pos < n)) | (col >= 3 * WINDOW)
        rows = slice(j * WINDOW, (j + 1) * WINDOW)
        q = jnp.concatenate([q_ref[0, rows, r * HEAD:(r + 1) * HEAD] for r in range(R)], axis=0)
        s = jnp.where(valid, _dot_nt(q, k) * scale, NEG_INF)
        o = _sink_softmax_pv(s, sk, v).astype(o_ref.dtype)
        for r in range(R):
            o_ref[0, rows, r * HEAD:(r + 1) * HEAD] = o[r * WINDOW:(r + 1) * WINDOW]


def _win_attn(qkv, qkv_ctx, sink, H, G):
    B, n, _ = qkv.shape
    m = qkv_ctx.shape[1]
    R = H // G
    assert n % WINDOW == 0
    tq = _tile(n, TW, WINDOW)
    nsub = tq // WINDOW
    nb = n // tq
    nw = n // WINDOW
    prev = lambda i: jnp.maximum(i * nsub - 1, 0)
    nxt = lambda i: jnp.minimum((i + 1) * nsub, nw - 1)
    ident = lambda i: i
    kspec = lambda f, rows, off: pl.BlockSpec((1, rows, HEAD), lambda b, g, i: (b, f(i), off + g))
    xspec = lambda off: pl.BlockSpec((1, m, HEAD), lambda b, g, i: (b, 0, off + g))
    return pl.pallas_call(
        functools.partial(_win_attn_kernel, tq=tq, n=n, R=R),
        grid=(B, G, nb),
        in_specs=[pl.BlockSpec(memory_space=pltpu.SMEM),
                  pl.BlockSpec((1, tq, R * HEAD), lambda b, g, i: (b, i, g)),
                  kspec(prev, WINDOW, H), kspec(ident, tq, H), kspec(nxt, WINDOW, H), xspec(H),
                  kspec(prev, WINDOW, H + G), kspec(ident, tq, H + G), kspec(nxt, WINDOW, H + G), xspec(H + G)],
        out_specs=pl.BlockSpec((1, tq, R * HEAD), lambda b, g, i: (b, i, g)),
        out_shape=jax.ShapeDtypeStruct((B, n, H * HEAD), BF16),
        compiler_params=_params("parallel", "parallel", "parallel"),
        name="win_attn",
    )(sink.astype(F32), qkv, qkv, qkv, qkv, qkv_ctx, qkv, qkv, qkv, qkv_ctx)


def _sink_attn_kernel(sink_ref, q_ref, k_ref, v_ref, o_ref, *, R):
    g = pl.program_id(1)
    q, k, v = q_ref[0], k_ref[0], v_ref[0]
    scale = HEAD ** -0.5
    for r in range(R):
        sl = slice(r * HEAD, (r + 1) * HEAD)
        s = _dot_nt(q[:, sl], k) * scale
        o_ref[0, :, sl] = _sink_softmax_pv(s, sink_ref[g * R + r], v).astype(o_ref.dtype)


def _sink_attn(qkv, sink, H, G):
    B, m, _ = qkv.shape
    R = H // G
    return pl.pallas_call(
        functools.partial(_sink_attn_kernel, R=R),
        grid=(B, G),
        in_specs=[pl.BlockSpec(memory_space=pltpu.SMEM),
                  pl.BlockSpec((1, m, R * HEAD), lambda b, g: (b, 0, g)),
                  pl.BlockSpec((1, m, HEAD), lambda b, g: (b, 0, H + g)),
                  pl.BlockSpec((1, m, HEAD), lambda b, g: (b, 0, H + G + g))],
        out_specs=pl.BlockSpec((1, m, R * HEAD), lambda b, g: (b, 0, g)),
        out_shape=jax.ShapeDtypeStruct((B, m, H * HEAD), BF16),
        compiler_params=_params("parallel", "parallel"),
        name="sink_attn",
    )(sink.astype(F32), qkv, qkv, qkv)


def _silu(g):
    return g * jax.nn.sigmoid(g)


def _ffn_kernel(x_ref, sh_ref, sc_ref, gate_ref, w1_ref, w3_ref, w2_ref, g_ref, b_ref, o_ref, h_ref, acc_ref,
                *, alpha, nf):
    f = pl.program_id(2)

    @pl.when(f == 0)
    def _():
        h_ref[...] = (x_ref[0] * (1.0 + sc_ref[0]) + sh_ref[0]).astype(BF16)
        acc_ref[...] = jnp.zeros(acc_ref.shape, F32)

    h = h_ref[...]
    a = _silu(jnp.dot(h, w1_ref[0], preferred_element_type=F32)) * jnp.dot(h, w3_ref[0], preferred_element_type=F32)
    acc_ref[...] += jnp.dot(a.astype(BF16), w2_ref[0], preferred_element_type=F32)

    @pl.when(f == nf - 1)
    def _():
        o_ref[0] = _res_ln(x_ref[0], acc_ref[...], gate_ref[0], g_ref[...], b_ref[...], alpha)


def _ffn(x, w13, w2, slot, mod, g, b, *, ctx, alpha):
    B, R, D = x.shape
    Fd = w2.shape[1]
    tm = _tile(R, TM)
    tf = _tile(Fd, TF, LANE)
    nf = Fd // tf
    row = _mod_row(ctx, B)
    vec = pl.BlockSpec((1, 1, D), lambda b, i, f: (row(b), 0, 0))
    return pl.pallas_call(
        functools.partial(_ffn_kernel, alpha=alpha, nf=nf),
        grid=(B, R // tm, nf),
        in_specs=[pl.BlockSpec((1, tm, D), lambda b, i, f: (b, i, 0)), vec, vec, vec,
                  pl.BlockSpec((1, D, tf), lambda b, i, f: (slot, 0, f)),
                  pl.BlockSpec((1, D, tf), lambda b, i, f: (slot, 0, nf + f)),
                  pl.BlockSpec((1, tf, D), lambda b, i, f: (slot, f, 0)),
                  pl.BlockSpec((1, D), lambda b, i, f: (0, 0)),
                  pl.BlockSpec((1, D), lambda b, i, f: (0, 0))],
        out_specs=pl.BlockSpec((1, tm, D), lambda b, i, f: (b, i, 0)),
        out_shape=jax.ShapeDtypeStruct((B, R, D), F32),
        scratch_shapes=[pltpu.VMEM((tm, D), BF16), pltpu.VMEM((tm, D), F32)],
        compiler_params=_params("parallel", "parallel", "arbitrary"),
        name="ffn",
    )(x, mod[0], mod[1], mod[2], w13, w13, w2, g.reshape(1, D), b.reshape(1, D))


def _router_kernel(x_ref, sh_ref, sc_ref, rt_ref, h_ref, idx_ref, wts_ref):
    hf = x_ref[0] * (1.0 + sc_ref[0]) + sh_ref[0]
    h_ref[0] = hf
    logits = jnp.dot(hf, rt_ref[...], preferred_element_type=F32, precision=lax.Precision.HIGHEST)
    E = logits.shape[1]
    lane = lax.broadcasted_iota(jnp.int32, logits.shape, 1)
    m1 = jnp.max(logits, axis=1, keepdims=True)
    i1 = jnp.min(jnp.where(logits == m1, lane, E), axis=1, keepdims=True)
    rest = jnp.where(lane == i1, -jnp.inf, logits)
    m2 = jnp.max(rest, axis=1, keepdims=True)
    i2 = jnp.min(jnp.where(rest == m2, lane, E), axis=1, keepdims=True)
    w1 = 1.0 / (1.0 + jnp.exp(m2 - m1))
    two = lax.broadcasted_iota(jnp.int32, idx_ref.shape[1:], 1)
    idx_ref[0] = jnp.where(two == 0, i1, i2)
    wts_ref[0] = jnp.where(two == 0, w1, 1.0 - w1)


def _router(x, router, mod, *, ctx):
    B, R, D = x.shape
    E = router.shape[1]
    tm = _tile(R, TM)
    row = _mod_row(ctx, B)
    vec = pl.BlockSpec((1, 1, D), lambda b, i: (row(b), 0, 0))
    return pl.pallas_call(
        _router_kernel,
        grid=(B, R // tm),
        in_specs=[pl.BlockSpec((1, tm, D), lambda b, i: (b, i, 0)), vec, vec,
                  pl.BlockSpec((D, E), lambda b, i: (0, 0))],
        out_specs=[pl.BlockSpec((1, tm, D), lambda b, i: (b, i, 0)),
                   pl.BlockSpec((1, tm, 2), lambda b, i: (b, i, 0)),
                   pl.BlockSpec((1, tm, 2), lambda b, i: (b, i, 0))],
        out_shape=[jax.ShapeDtypeStruct((B, R, D), F32),
                   jax.ShapeDtypeStruct((B, R, 2), jnp.int32),
                   jax.ShapeDtypeStruct((B, R, 2), F32)],
        compiler_params=_params("parallel", "parallel"),
        name="router",
    )(x, mod[0], mod[1], router)


def _route_plan(idx, wts, E, tm):
    T = idx.shape[0]
    P = 2 * T
    e_flat = idx.reshape(P)
    onehot = (e_flat[:, None] == jnp.arange(E, dtype=jnp.int32)[None, :]).astype(jnp.int32)
    counts = jnp.sum(onehot, axis=0)
    rank = jnp.take_along_axis(jnp.cumsum(onehot, axis=0) - onehot, e_flat[:, None], axis=1)[:, 0]
    padded = ((counts + tm - 1) // tm) * tm
    ends = jnp.cumsum(padded)
    starts = ends - padded
    slot = starts[e_flat] + rank
    ntiles = P // tm + E
    S = ntiles * tm
    tile_start = jnp.arange(ntiles, dtype=jnp.int32) * tm
    tile_expert = jnp.minimum(jnp.searchsorted(ends, tile_start, side="right"), E - 1).astype(jnp.int32)
    tile_rows = jnp.clip(counts[tile_expert] - (tile_start - starts[tile_expert]), 0, tm)
    tile_rows = jnp.where(tile_start < ends[E - 1], tile_rows, 0).astype(jnp.int32)
    last = jnp.maximum(jnp.sum((tile_rows > 0).astype(jnp.int32)) - 1, 0)
    tile_expert = jnp.where(tile_rows > 0, tile_expert, tile_expert[last])
    pair = jnp.zeros((S,), jnp.int32).at[slot].set(jnp.arange(P, dtype=jnp.int32))
    token, choice = pair // 2, pair % 2
    src = token
    dst = choice * T + token
    gate = wts.reshape(P)[pair]
    return tile_expert, tile_rows, src.reshape(ntiles, 1, tm), dst.reshape(ntiles, 1, tm), gate.reshape(S, 1)


def _experts_kernel(te_ref, tr_ref, src_ref, srcn_ref, dst_ref, gate_ref, h_hbm, w1_ref, w3_ref, w2_ref, y_hbm,
                    rows_ref, hb_ref, acc_ref, sem_in, sem_out, *, nf, tm, ntiles, gsteps):
    t = pl.program_id(0)
    f = pl.program_id(1)
    nrows = tr_ref[t]
    valid = nrows > 0
    slot = t % 2
    chunk = tm // gsteps

    def gather_row(idx_ref, r, s):
        return pltpu.make_async_copy(h_hbm.at[pl.ds(idx_ref[0, 0, r], 1)], rows_ref.at[s, pl.ds(r, 1)], sem_in)

    def scatter_wait(n, s):
        nfull = pl.multiple_of((n // SUBLANE) * SUBLANE, SUBLANE)

        @pl.when(nfull > 0)
        def _():
            pltpu.make_async_copy(rows_ref.at[s, pl.ds(0, nfull)], y_hbm.at[pl.ds(0, nfull)], sem_out).wait()

        def single(r, carry):
            pltpu.make_async_copy(rows_ref.at[s, pl.ds(r, 1)], y_hbm.at[pl.ds(0, 1)], sem_out).wait()
            return carry

        lax.fori_loop(nfull, n, single, 0)

    @pl.when((t == 0) & (f == 0))
    def _():
        def group(i, carry):
            for u in range(ROW_UNROLL):
                gather_row(src_ref, i * ROW_UNROLL + u, 0).start()
            return carry

        lax.fori_loop(0, tm // ROW_UNROLL, group, 0)

    @pl.when(valid & (f == 0))
    def _():
        pltpu.make_async_copy(h_hbm.at[pl.ds(0, tm)], rows_ref.at[slot], sem_in).wait()
        hb_ref[...] = rows_ref[slot].astype(BF16)
        acc_ref[...] = jnp.zeros(acc_ref.shape, F32)

    @pl.when((f == 1) & (t >= 1))
    def _():
        scatter_wait(tr_ref[jnp.maximum(t - 1, 0)], 1 - slot)

    def matmuls():
        h = hb_ref[...]
        a = _silu(jnp.dot(h, w1_ref[0], preferred_element_type=F32)) * jnp.dot(h, w3_ref[0], preferred_element_type=F32)
        acc_ref[...] += jnp.dot(a.astype(BF16), w2_ref[0], preferred_element_type=F32)

    def prefetch_chunk():
        base = (f - 1) * chunk
        for u in range(chunk):
            gather_row(srcn_ref, base + u, 1 - slot).start()

    prefetch = (tr_ref[jnp.minimum(t + 1, ntiles - 1)] > 0) & (t + 1 < ntiles) & (f >= 1) & (f <= gsteps)

    @pl.when(valid & prefetch)
    def _():
        matmuls()
        prefetch_chunk()

    @pl.when(valid & jnp.logical_not(prefetch))
    def _():
        matmuls()

    @pl.when(jnp.logical_not(valid) & prefetch)
    def _():
        prefetch_chunk()

    @pl.when(valid & (f == nf - 1))
    def _():
        rows_ref[slot] = acc_ref[...] * gate_ref[...]

        def scatter_row(r):
            return pltpu.make_async_copy(rows_ref.at[slot, pl.ds(r, 1)],
                                         y_hbm.at[pl.ds(dst_ref[0, 0, r], 1)], sem_out)

        ngroups = nrows // ROW_UNROLL

        def group(i, carry):
            for u in range(ROW_UNROLL):
                scatter_row(i * ROW_UNROLL + u).start()
            return carry

        def single(r, carry):
            scatter_row(r).start()
            return carry

        lax.fori_loop(0, ngroups, group, 0)
        lax.fori_loop(ngroups * ROW_UNROLL, nrows, single, 0)

    @pl.when(valid & (f == nf - 1) & (t == ntiles - 1))
    def _():
        scatter_wait(nrows, slot)


def _experts(h, plan, w13, w2, e_off, tm):
    T, D = h.shape
    Fd = w2.shape[1]
    tile_expert, tile_rows, src, dst, gate = plan
    tile_expert = tile_expert + e_off
    ntiles = src.shape[0]
    tf = _tile(Fd, TF, LANE)
    nf = Fd // tf
    assert nf >= 2, "the gather of the next tile is spread over the steps after the first"
    gsteps = 1
    while gsteps * 2 <= nf - 1 and tm % (gsteps * 2) == 0:
        gsteps *= 2
    fidx = lambda t, f, tv: jnp.where(tv[t] > 0, f, nf - 1)
    grid_spec = pltpu.PrefetchScalarGridSpec(
        num_scalar_prefetch=2,
        grid=(ntiles, nf),
        in_specs=[pl.BlockSpec((1, 1, tm), lambda t, f, te, tv: (t, 0, 0), memory_space=pltpu.SMEM),
                  pl.BlockSpec((1, 1, tm), lambda t, f, te, tv: (jnp.minimum(t + 1, ntiles - 1), 0, 0),
                               memory_space=pltpu.SMEM),
                  pl.BlockSpec((1, 1, tm), lambda t, f, te, tv: (t, 0, 0), memory_space=pltpu.SMEM),
                  pl.BlockSpec((tm, 1), lambda t, f, te, tv: (t, 0)),
                  pl.BlockSpec(memory_space=pl.ANY),
                  pl.BlockSpec((1, D, tf), lambda t, f, te, tv: (te[t], 0, fidx(t, f, tv))),
                  pl.BlockSpec((1, D, tf), lambda t, f, te, tv: (te[t], 0, nf + fidx(t, f, tv))),
                  pl.BlockSpec((1, tf, D), lambda t, f, te, tv: (te[t], fidx(t, f, tv), 0))],
        out_specs=pl.BlockSpec(memory_space=pl.ANY),
        scratch_shapes=[pltpu.VMEM((2, tm, D), F32), pltpu.VMEM((tm, D), BF16), pltpu.VMEM((tm, D), F32),
                        pltpu.SemaphoreType.DMA(()), pltpu.SemaphoreType.DMA(())],
    )
    return pl.pallas_call(
        functools.partial(_experts_kernel, nf=nf, tm=tm, ntiles=ntiles, gsteps=gsteps),
        grid_spec=grid_spec,
        out_shape=jax.ShapeDtypeStruct((2 * T, D), F32),
        compiler_params=pltpu.CompilerParams(dimension_semantics=("arbitrary", "arbitrary"),
                                             vmem_limit_bytes=VMEM_LIMIT),
        name="experts",
    )(tile_expert, tile_rows, src, src, dst, gate, h, w13, w13, w2)


def _combine_kernel(x_ref, y0_ref, y1_ref, gate_ref, g_ref, b_ref, o_ref, *, alpha):
    o_ref[0] = _res_ln(x_ref[0], y0_ref[...] + y1_ref[...], gate_ref[0], g_ref[...], b_ref[...], alpha)


def _combine_ln(x, y, gate, g, b, *, ctx, alpha):
    B, R, D = x.shape
    tm = _tile(R, TM)
    nb = R // tm
    row = _mod_row(ctx, B)
    return pl.pallas_call(
        functools.partial(_combine_kernel, alpha=alpha),
        grid=(B, nb),
        in_specs=[pl.BlockSpec((1, tm, D), lambda b, i: (b, i, 0)),
                  pl.BlockSpec((tm, D), lambda b, i: (b * nb + i, 0)),
                  pl.BlockSpec((tm, D), lambda b, i: (B * nb + b * nb + i, 0)),
                  pl.BlockSpec((1, 1, D), lambda b, i: (row(b), 0, 0)),
                  pl.BlockSpec((1, D), lambda b, i: (0, 0)),
                  pl.BlockSpec((1, D), lambda b, i: (0, 0))],
        out_specs=pl.BlockSpec((1, tm, D), lambda b, i: (b, i, 0)),
        out_shape=jax.ShapeDtypeStruct((B, R, D), F32),
        compiler_params=_params("parallel", "parallel"),
        name="combine_ln",
    )(x, y, y, gate, g.reshape(1, D), b.reshape(1, D))


def _moe(x, router, w13, w2, slot, mod, g, b, *, ctx, alpha):
    B, R, D = x.shape
    E = router.shape[1]
    T = B * R
    tm = _tile(2 * T, TME)
    h, idx, wts = _router(x, router, mod, ctx=ctx)
    plan = _route_plan(idx.reshape(T, 2), wts.reshape(T, 2), E, tm)
    y = _experts(h.reshape(T, D), plan, w13, w2, slot * E, tm)
    return _combine_ln(x, y, mod[2], g, b, ctx=ctx, alpha=alpha)


def _mix_diff(xl, xc, mod, need_ctx, layer, wqkv, lam, subln_g, rope128):
    D = xl.shape[2]
    H = D // (2 * HEAD)
    lam_init = 0.8 - 0.6 * math.exp(-0.3 * layer)
    w = wqkv.astype(BF16)
    qs = HEAD ** -0.5 * LOG2E
    qkv_l = _proj(xl, w, mod=mod, ctx=False, rope=rope128, rope_cols=(0, 2 * D), q_cols=D, q_scale=qs)
    qkv_c = _proj(xc, w, mod=mod, ctx=True, q_cols=D, q_scale=qs)
    kv_all = jnp.concatenate([qkv_l, qkv_c], axis=1)
    o_l = _diff_attn(qkv_l, kv_all, lam, subln_g, lam_init, H)
    o_c = _diff_attn(qkv_c, qkv_c, lam, subln_g, lam_init, H) if need_ctx else None
    return o_l, o_c


def _mix_window(xl, xc, mod, need_ctx, wqkv, sink, rope128):
    D = xl.shape[2]
    H = D // HEAD
    G = H // 4
    w = wqkv.astype(BF16)
    qkv_l = _proj(xl, w, mod=mod, ctx=False, rope=rope128, rope_cols=(0, (H + G) * HEAD))
    qkv_c = _proj(xc, w, mod=mod, ctx=True)
    o_l = _win_attn(qkv_l, qkv_c, sink, H, G)
    o_c = _sink_attn(qkv_c, sink, H, G) if need_ctx else None
    return o_l, o_c


def _mix_mla(xl, xc, mod, need_ctx, w_down, gq, gkv, w_uq, w_ukv, rope64):
    D = xl.shape[2]
    H = D // HEAD
    rq, rkv = gq.shape[0], gkv.shape[0]
    wd = jnp.pad(w_down, ((0, 0), (0, LANE - MLA_ROPE))).astype(BF16)
    wq3 = w_uq.reshape(rq, H, HEAD + MLA_ROPE)
    wq = jnp.concatenate(
        [wq3[:, :, :HEAD].reshape(rq, H * HEAD),
         jnp.pad(wq3[:, :, HEAD:], ((0, 0), (0, 0), (0, HEAD - MLA_ROPE))).reshape(rq, H * HEAD)], axis=1).astype(BF16)
    wkv3 = w_ukv.reshape(rkv, H, 2 * HEAD)
    wkv = jnp.concatenate([wkv3[:, :, :HEAD].reshape(rkv, H * HEAD),
                           wkv3[:, :, HEAD:].reshape(rkv, H * HEAD)], axis=1).astype(BF16)
    cq_l, ckv_l, kr_l = _mla_down(xl, wd, gq, gkv, mod, ctx=False, rope=rope64)
    cq_c, ckv_c, kr_c = _mla_down(xc, wd, gq, gkv, mod, ctx=True, rope=None)
    qs = (HEAD + MLA_ROPE) ** -0.5 * LOG2E
    q_l = _proj(cq_l, wq, rope=rope64, rope_cols=(H * HEAD, 2 * H * HEAD), q_cols=2 * H * HEAD, q_scale=qs)
    kv_l = _proj(ckv_l, wkv)
    kv_c = _proj(ckv_c, wkv)
    kv_all = jnp.concatenate([kv_l, kv_c], axis=1)
    kr_all = jnp.concatenate([kr_l, kr_c], axis=1)
    o_l = _mla_attn(q_l, kv_all, kr_all, H)
    o_c = None
    if need_ctx:
        q_c = _proj(cq_c, wq, q_cols=2 * H * HEAD, q_scale=qs)
        o_c = _mla_attn(q_c, kv_c, kr_c, H)
    return o_l, o_c


def kernel(x, c, ctx, c_ctx, mod_w, mod_b, ln1_g, ln1_b, ln2_g, ln2_b, da_wqkv, da_lambda, da_subln_g, da_wo, wa_wqkv, wa_sink, wa_wo, mla_w_down, mla_q_norm_g, mla_kv_norm_g, mla_w_uq, mla_w_ukv, mla_wo, ffn_w13, ffn_w2, moe_router, moe_w13, moe_w2):
    B, n, D = x.shape
    depth = mod_w.shape[0]
    alpha = (2.0 * depth) ** 0.25
    rows = SUBLANE * (-(-(B + 1) // SUBLANE))
    cc = jnp.concatenate([c, c_ctx[None, :], jnp.zeros((rows - B - 1, D), F32)], axis=0)
    mod_all = _modulation(cc, mod_w, mod_b)
    rope128 = _rope_tables(n, HEAD)
    rope64 = _rope_tables(n, MLA_ROPE)
    ffn_w13_b, ffn_w2_b = ffn_w13.astype(BF16), ffn_w2.astype(BF16)
    moe_w13_b = moe_w13.astype(BF16).reshape((-1,) + moe_w13.shape[2:])
    moe_w2_b = moe_w2.astype(BF16).reshape((-1,) + moe_w2.shape[2:])
    xl, xc = x, ctx
    for i in range(depth):
        need_ctx = i < depth - 1
        mod = [mod_all[i, :, k * D:(k + 1) * D].reshape(rows, 1, D) for k in range(6)]
        kind, slot = i % N_MIXERS, i // N_MIXERS
        if kind == 0:
            o_l, o_c = _mix_diff(xl, xc, mod[0:2], need_ctx, i, da_wqkv[slot], da_lambda[slot],
                                 da_subln_g[slot], rope128)
            wo = da_wo[slot]
        elif kind == 1:
            o_l, o_c = _mix_window(xl, xc, mod[0:2], need_ctx, wa_wqkv[slot], wa_sink[slot], rope128)
            wo = wa_wo[slot]
        else:
            o_l, o_c = _mix_mla(xl, xc, mod[0:2], need_ctx, mla_w_down[slot], mla_q_norm_g[slot],
                                mla_kv_norm_g[slot], mla_w_uq[slot], mla_w_ukv[slot], rope64)
            wo = mla_wo[slot]
        wo = wo.astype(BF16)
        xl = _oproj_ln(xl, o_l, wo, mod[2], ln1_g[i], ln1_b[i], ctx=False, alpha=alpha)
        if need_ctx:
            xc = _oproj_ln(xc, o_c, wo, mod[2], ln1_g[i], ln1_b[i], ctx=True, alpha=alpha)
        cslot = i // 2
        if i % 2 == 0:
            xl = _ffn(xl, ffn_w13_b, ffn_w2_b, cslot, mod[3:6], ln2_g[i], ln2_b[i], ctx=False, alpha=alpha)
            if need_ctx:
                xc = _ffn(xc, ffn_w13_b, ffn_w2_b, cslot, mod[3:6], ln2_g[i], ln2_b[i], ctx=True, alpha=alpha)
        else:
            xl = _moe(xl, moe_router[cslot], moe_w13_b, moe_w2_b, cslot, mod[3:6], ln2_g[i], ln2_b[i],
                      ctx=False, alpha=alpha)
            if need_ctx:
                xc = _moe(xc, moe_router[cslot], moe_w13_b, moe_w2_b, cslot, mod[3:6], ln2_g[i], ln2_b[i],
                          ctx=True, alpha=alpha)
    return xl
```

```python
import functools
import math

import jax
import jax.numpy as jnp
from jax import lax
from jax.experimental import pallas as pl
from jax.experimental.pallas import tpu as pltpu

F32 = jnp.float32
BF16 = jnp.bfloat16

LANE = 128
SUBLANE = 8
VMEM_LIMIT = 56 * 1024 * 1024

GRID_W = 64
ROPE_BASE = 10000.0
LN_EPS = 1e-5
RMS_EPS = 1e-6
NEG_INF = -1e30
HEAD = 128
WINDOW = 128
MLA_ROPE = 64
N_MIXERS = 3
LOG2E = math.log2(math.e)

TM = 512
TMP = 1024
TW = 512
TN = 1024
TF = 512
TQ = 2048
TK = 2816
RB = 256
TME = 1024
ROW_UNROLL = 8


def _tile(dim, pref, align=SUBLANE):
    if dim <= pref:
        return dim
    t = (pref // align) * align
    while t >= align:
        if dim % t == 0:
            return t
        t -= align
    return dim


def _params(*sem):
    return pltpu.CompilerParams(dimension_semantics=sem, vmem_limit_bytes=VMEM_LIMIT)


def _mod_row(ctx, nb):
    return (lambda b: nb) if ctx else (lambda b: b)


def _mod_kernel(s_ref, w_ref, b_ref, o_ref):
    s = s_ref[...]
    s = s * jax.nn.sigmoid(s)
    o_ref[0] = jnp.dot(s.astype(BF16), w_ref[0].astype(BF16), preferred_element_type=F32) + b_ref[0]


def _modulation(cc, mod_w, mod_b):
    depth, D, N = mod_w.shape
    R = cc.shape[0]
    tn = _tile(N, 1024, LANE)
    return pl.pallas_call(
        _mod_kernel,
        grid=(depth, N // tn),
        in_specs=[pl.BlockSpec((R, D), lambda l, j: (0, 0)),
                  pl.BlockSpec((1, D, tn), lambda l, j: (l, 0, j)),
                  pl.BlockSpec((1, 1, tn), lambda l, j: (l, 0, j))],
        out_specs=pl.BlockSpec((1, R, tn), lambda l, j: (l, 0, j)),
        out_shape=jax.ShapeDtypeStruct((depth, R, N), F32),
        compiler_params=_params("parallel", "parallel"),
        name="modulation",
    )(cc, mod_w, mod_b.reshape(depth, 1, N))


def _rope_tables(n, dim):
    half = dim // 4
    t = jnp.arange(n)
    row = (t // GRID_W).astype(F32)
    col = (t % GRID_W).astype(F32)
    inv = ROPE_BASE ** (-jnp.arange(half, dtype=F32) / half)
    ar = row[:, None] * inv[None, :]
    ac = col[:, None] * inv[None, :]
    cos = jnp.concatenate([jnp.cos(ar), jnp.cos(ar), jnp.cos(ac), jnp.cos(ac)], axis=1)
    sin = jnp.concatenate([-jnp.sin(ar), jnp.sin(ar), -jnp.sin(ac), jnp.sin(ac)], axis=1)
    rep = LANE // dim
    return jnp.tile(cos, (1, rep)), jnp.tile(sin, (1, rep)), half


def _rope_chunk(yc, cos, sin, half):
    lane = lax.broadcasted_iota(jnp.int32, (1, LANE), 1)
    first = (lane % (2 * half)) < half
    rot = jnp.where(first, pltpu.roll(yc, LANE - half, 1), pltpu.roll(yc, half, 1))
    return yc * cos + rot * sin


def _proj_kernel(*refs, modulate, rope_lo, rope_hi, half, q_hi, q_scale):
    refs = list(refs)
    x_ref = refs.pop(0)
    if modulate:
        sh_ref, sc_ref = refs.pop(0), refs.pop(0)
    w_ref = refs.pop(0)
    if rope_hi > rope_lo:
        cos_ref, sin_ref = refs.pop(0), refs.pop(0)
    o_ref = refs.pop(0)
    j = pl.program_id(2)
    if modulate:
        h_ref = refs.pop(0)

        @pl.when(j == 0)
        def _():
            h_ref[...] = (x_ref[0] * (1.0 + sc_ref[0]) + sh_ref[0]).astype(BF16)

        h = h_ref[...]
    else:
        h = x_ref[0]
    y = jnp.dot(h, w_ref[...], preferred_element_type=F32)
    if q_hi > 0:
        y = y * jnp.where(j < q_hi, q_scale, 1.0)
    if rope_hi > rope_lo:
        in_rope = (j >= rope_lo) & (j < rope_hi)

        @pl.when(in_rope)
        def _():
            cos, sin = cos_ref[...], sin_ref[...]
            for cidx in range(y.shape[1] // LANE):
                sl = slice(cidx * LANE, (cidx + 1) * LANE)
                o_ref[0, :, sl] = _rope_chunk(y[:, sl], cos, sin, half).astype(o_ref.dtype)

        @pl.when(jnp.logical_not(in_rope))
        def _():
            o_ref[0] = y.astype(o_ref.dtype)
    else:
        o_ref[0] = y.astype(o_ref.dtype)


def _proj(x, w, *, mod=None, ctx=False, rope=None, rope_cols=(0, 0), q_cols=0, q_scale=1.0):
    B, R, K = x.shape
    N = w.shape[1]
    tm = _tile(R, TMP)
    g = math.gcd(N, q_cols)
    if rope is not None:
        g = math.gcd(g, math.gcd(rope_cols[0], rope_cols[1]))
    tn = _tile(g, TN, LANE)
    modulate = mod is not None
    args = [x]
    in_specs = [pl.BlockSpec((1, tm, K), lambda b, i, j: (b, i, 0))]
    if modulate:
        shift, scale = mod
        row = _mod_row(ctx, B)
        in_specs += [pl.BlockSpec((1, 1, K), lambda b, i, j: (row(b), 0, 0))] * 2
        args += [shift, scale]
    in_specs.append(pl.BlockSpec((K, tn), lambda b, i, j: (0, j)))
    args.append(w)
    half = 0
    if rope is not None:
        cos, sin, half = rope
        in_specs += [pl.BlockSpec((tm, LANE), lambda b, i, j: (i, 0))] * 2
        args += [cos, sin]
    kern = functools.partial(_proj_kernel, modulate=modulate, rope_lo=rope_cols[0] // tn,
                             rope_hi=rope_cols[1] // tn if rope is not None else 0, half=half,
                             q_hi=q_cols // tn, q_scale=q_scale)
    return pl.pallas_call(
        kern,
        grid=(B, R // tm, N // tn),
        in_specs=in_specs,
        out_specs=pl.BlockSpec((1, tm, tn), lambda b, i, j: (b, i, j)),
        out_shape=jax.ShapeDtypeStruct((B, R, N), BF16),
        scratch_shapes=[pltpu.VMEM((tm, K), BF16)] if modulate else [],
        compiler_params=_params("parallel", "parallel", "arbitrary"),
        name="proj",
    )(*args)


def _res_ln(x, y, gate, g, b, alpha):
    z = alpha * x + gate * y
    mu = jnp.mean(z, axis=-1, keepdims=True)
    zc = z - mu
    var = jnp.mean(zc * zc, axis=-1, keepdims=True)
    return zc * lax.rsqrt(var + LN_EPS) * g + b


def _oproj_kernel(x_ref, o_ref, w_ref, gate_ref, g_ref, b_ref, out_ref, *, alpha):
    y = jnp.dot(o_ref[0], w_ref[...], preferred_element_type=F32)
    out_ref[0] = _res_ln(x_ref[0], y, gate_ref[0], g_ref[...], b_ref[...], alpha)


def _oproj_ln(x, o, w, gate, g, b, *, ctx, alpha):
    B, R, D = x.shape
    K = o.shape[2]
    tm = _tile(R, TM)
    row = _mod_row(ctx, B)
    return pl.pallas_call(
        functools.partial(_oproj_kernel, alpha=alpha),
        grid=(B, R // tm),
        in_specs=[pl.BlockSpec((1, tm, D), lambda b, i: (b, i, 0)),
                  pl.BlockSpec((1, tm, K), lambda b, i: (b, i, 0)),
                  pl.BlockSpec((K, D), lambda b, i: (0, 0)),
                  pl.BlockSpec((1, 1, D), lambda b, i: (row(b), 0, 0)),
                  pl.BlockSpec((1, D), lambda b, i: (0, 0)),
                  pl.BlockSpec((1, D), lambda b, i: (0, 0))],
        out_specs=pl.BlockSpec((1, tm, D), lambda b, i: (b, i, 0)),
        out_shape=jax.ShapeDtypeStruct((B, R, D), F32),
        compiler_params=_params("parallel", "parallel"),
        name="oproj_ln",
    )(x, o, w, gate, g.reshape(1, D), b.reshape(1, D))


def _flash_init(m_ref, l_ref, acc_ref):
    m_ref[...] = jnp.full(m_ref.shape, NEG_INF, F32)
    l_ref[...] = jnp.zeros(l_ref.shape, F32)
    acc_ref[...] = jnp.zeros(acc_ref.shape, F32)


def _dot_nt(a, b):
    return lax.dot_general(a, b, (((1,), (1,)), ((), ())), preferred_element_type=F32)


def _flash_step(q_fn, k_fn, v_fn, m_ref, l_ref, acc_ref, *, chains, tq):
    rb = min(RB, tq)

    def body(r, carry):
        rows = pl.ds(pl.multiple_of(r * rb, rb), rb)
        for c in range(chains):
            s = _dot_nt(q_fn(c, rows), k_fn(c))
            m_prev = m_ref[c, rows]
            m_new = jnp.maximum(m_prev, jnp.max(s, axis=1, keepdims=True))
            a = jnp.exp2(m_prev - m_new)
            p = jnp.exp2(s - m_new)
            l_ref[c, rows] = a * l_ref[c, rows] + jnp.sum(p, axis=1, keepdims=True)
            acc_ref[c, rows] = a * acc_ref[c, rows] + jnp.dot(p.astype(BF16), v_fn(c), preferred_element_type=F32)
            m_ref[c, rows] = m_new
        return carry

    lax.fori_loop(0, tq // rb, body, 0, unroll=True)


def _stage_keys(ki, nk, tail, pairs):
    for lat_ref, ctx_ref, buf_ref in pairs:
        if tail is None:
            buf_ref[...] = lat_ref[0]
            continue

        @pl.when(ki < nk - 1)
        def _():
            buf_ref[...] = lat_ref[0]

        @pl.when(ki == nk - 1)
        def _():
            buf_ref[:tail] = lat_ref[0, :tail]
            buf_ref[tail:] = ctx_ref[0]


def _diff_attn_kernel(lam_ref, g_ref, q_ref, k_ref, v_ref, *rest, lam_init, nk, tail):
    if tail is None:
        kc_ref = vc_ref = None
        o_ref, m_ref, l_ref, acc_ref, kbuf, vbuf = rest
    else:
        kc_ref, vc_ref, o_ref, m_ref, l_ref, acc_ref, kbuf, vbuf = rest
    ki = pl.program_id(3)

    @pl.when(ki == 0)
    def _():
        _flash_init(m_ref, l_ref, acc_ref)

    _stage_keys(ki, nk, tail, [(k_ref, kc_ref, kbuf), (v_ref, vc_ref, vbuf)])
    _flash_step(lambda c, rows: q_ref[0, rows, c * HEAD:(c + 1) * HEAD],
                lambda c: kbuf[:, c * HEAD:(c + 1) * HEAD],
                lambda c: vbuf[...],
                m_ref, l_ref, acc_ref, chains=2, tq=q_ref.shape[1])

    @pl.when(ki == nk - 1)
    def _():
        lam = lam_ref[...]
        lam_full = (jnp.exp(jnp.sum(lam[0:1] * lam[1:2], axis=1, keepdims=True))
                    - jnp.exp(jnp.sum(lam[2:3] * lam[3:4], axis=1, keepdims=True)) + lam_init)
        o = acc_ref[0] / l_ref[0] - lam_full * (acc_ref[1] / l_ref[1])
        ms = jnp.mean(o * o, axis=1, keepdims=True)
        o_ref[0] = (o * lax.rsqrt(ms + RMS_EPS) * g_ref[...] * (1.0 - lam_init)).astype(o_ref.dtype)


def _kv_tiling(n, m):
    tk = _tile(n + m, TK)
    nk = (n + m) // tk
    if m == 0:
        return tk, nk, None
    tail = n - (nk - 1) * tk
    assert 0 < tail and tail + m == tk and tail % 16 == 0, "context keys must complete the last kv block"
    return tk, nk, tail


def _diff_attn(qkv, qkv_ctx, lam, subln_g, lam_init, H):
    B, nq, _ = qkv.shape
    m = 0 if qkv_ctx is None else qkv_ctx.shape[1]
    W = 2 * HEAD
    tq = _tile(nq, TQ)
    tk, nk, tail = _kv_tiling(nq, m)
    in_specs = [pl.BlockSpec((4, HEAD), lambda b, h, i, k: (0, 0)),
                pl.BlockSpec((1, W), lambda b, h, i, k: (0, 0)),
                pl.BlockSpec((1, tq, W), lambda b, h, i, k: (b, i, h)),
                pl.BlockSpec((1, tk, W), lambda b, h, i, k: (b, k, H + h)),
                pl.BlockSpec((1, tk, W), lambda b, h, i, k: (b, k, 2 * H + h))]
    args = [lam.astype(F32), subln_g.reshape(1, W), qkv, qkv, qkv]
    if m:
        in_specs += [pl.BlockSpec((1, m, W), lambda b, h, i, k: (b, 0, H + h)),
                     pl.BlockSpec((1, m, W), lambda b, h, i, k: (b, 0, 2 * H + h))]
        args += [qkv_ctx, qkv_ctx]
    return pl.pallas_call(
        functools.partial(_diff_attn_kernel, lam_init=lam_init, nk=nk, tail=tail),
        grid=(B, H, nq // tq, nk),
        in_specs=in_specs,
        out_specs=pl.BlockSpec((1, tq, W), lambda b, h, i, k: (b, i, h)),
        out_shape=jax.ShapeDtypeStruct((B, nq, H * W), BF16),
        scratch_shapes=[pltpu.VMEM((2, tq, 1), F32), pltpu.VMEM((2, tq, 1), F32),
                        pltpu.VMEM((2, tq, W), F32), pltpu.VMEM((tk, W), BF16), pltpu.VMEM((tk, W), BF16)],
        compiler_params=_params("parallel", "parallel", "parallel", "arbitrary"),
        name="diff_attn",
    )(*args)


def _mla_attn_kernel(qn_ref, qr_ref, kn_ref, kr_ref, v_ref, *rest, nk, tail):
    if tail is None:
        knc_ref = krc_ref = vc_ref = None
        o_ref, m_ref, l_ref, acc_ref, k_ref, knbuf, krbuf, vbuf = rest
    else:
        knc_ref, krc_ref, vc_ref, o_ref, m_ref, l_ref, acc_ref, k_ref, knbuf, krbuf, vbuf = rest
    ki = pl.program_id(3)

    @pl.when(ki == 0)
    def _():
        _flash_init(m_ref, l_ref, acc_ref)

    _stage_keys(ki, nk, tail, [(kn_ref, knc_ref, knbuf), (kr_ref, krc_ref, krbuf), (v_ref, vc_ref, vbuf)])
    for c in range(2):
        k_ref[c, :, :HEAD] = knbuf[:, c * HEAD:(c + 1) * HEAD]
        k_ref[c, :, HEAD:] = krbuf[...]

    def q_fn(c, rows):
        sl = slice(c * HEAD, (c + 1) * HEAD)
        return jnp.concatenate([qn_ref[0, rows, sl], qr_ref[0, rows, sl]], axis=1)

    _flash_step(q_fn, lambda c: k_ref[c], lambda c: vbuf[:, c * HEAD:(c + 1) * HEAD],
                m_ref, l_ref, acc_ref, chains=2, tq=qn_ref.shape[1])

    @pl.when(ki == nk - 1)
    def _():
        for c in range(2):
            o_ref[0, :, c * HEAD:(c + 1) * HEAD] = (acc_ref[c] / l_ref[c]).astype(o_ref.dtype)


def _mla_attn(q_src, kv, kr, kv_ctx, kr_ctx, H):
    B, nq, _ = q_src.shape
    m = 0 if kv_ctx is None else kv_ctx.shape[1]
    assert H % 2 == 0
    P = H // 2
    W = 2 * HEAD
    tq = _tile(nq, TQ)
    tk, nk, tail = _kv_tiling(nq, m)
    in_specs = [pl.BlockSpec((1, tq, W), lambda b, h, i, k: (b, i, h)),
                pl.BlockSpec((1, tq, W), lambda b, h, i, k: (b, i, P + h)),
                pl.BlockSpec((1, tk, W), lambda b, h, i, k: (b, k, h)),
                pl.BlockSpec((1, tk, HEAD), lambda b, h, i, k: (b, k, 0)),
                pl.BlockSpec((1, tk, W), lambda b, h, i, k: (b, k, P + h))]
    args = [q_src, q_src, kv, kr, kv]
    if m:
        in_specs += [pl.BlockSpec((1, m, W), lambda b, h, i, k: (b, 0, h)),
                     pl.BlockSpec((1, m, HEAD), lambda b, h, i, k: (b, 0, 0)),
                     pl.BlockSpec((1, m, W), lambda b, h, i, k: (b, 0, P + h))]
        args += [kv_ctx, kr_ctx, kv_ctx]
    return pl.pallas_call(
        functools.partial(_mla_attn_kernel, nk=nk, tail=tail),
        grid=(B, P, nq // tq, nk),
        in_specs=in_specs,
        out_specs=pl.BlockSpec((1, tq, W), lambda b, h, i, k: (b, i, h)),
        out_shape=jax.ShapeDtypeStruct((B, nq, H * HEAD), BF16),
        scratch_shapes=[pltpu.VMEM((2, tq, 1), F32), pltpu.VMEM((2, tq, 1), F32),
                        pltpu.VMEM((2, tq, HEAD), F32), pltpu.VMEM((2, tk, W), BF16),
                        pltpu.VMEM((tk, W), BF16), pltpu.VMEM((tk, HEAD), BF16), pltpu.VMEM((tk, W), BF16)],
        compiler_params=_params("parallel", "parallel", "parallel", "arbitrary"),
        name="mla_attn",
    )(*args)


def _mla_down_kernel(x_ref, sh_ref, sc_ref, w_ref, gq_ref, gkv_ref, *rest, rq, rkv, rope):
    if rope:
        cos_ref, sin_ref, cq_ref, ckv_ref, kr_ref = rest
    else:
        cq_ref, ckv_ref, kr_ref = rest
    h = (x_ref[0] * (1.0 + sc_ref[0]) + sh_ref[0]).astype(BF16)
    y = jnp.dot(h, w_ref[...], preferred_element_type=F32)

    def rms(t, g):
        return t * lax.rsqrt(jnp.mean(t * t, axis=1, keepdims=True) + RMS_EPS) * g

    cq_ref[0] = rms(y[:, :rq], gq_ref[...]).astype(BF16)
    ckv_ref[0] = rms(y[:, rq:rq + rkv], gkv_ref[...]).astype(BF16)
    kr = y[:, rq + rkv:]
    if rope:
        kr = _rope_chunk(kr, cos_ref[...], sin_ref[...], MLA_ROPE // 4)
    kr_ref[0] = kr.astype(BF16)


def _mla_down(x, w, gq, gkv, mod, *, ctx, rope):
    B, R, D = x.shape
    rq, rkv = gq.shape[0], gkv.shape[0]
    N = w.shape[1]
    tm = _tile(R, TM)
    row = _mod_row(ctx, B)
    in_specs = [pl.BlockSpec((1, tm, D), lambda b, i: (b, i, 0)),
                pl.BlockSpec((1, 1, D), lambda b, i: (row(b), 0, 0)),
                pl.BlockSpec((1, 1, D), lambda b, i: (row(b), 0, 0)),
                pl.BlockSpec((D, N), lambda b, i: (0, 0)),
                pl.BlockSpec((1, rq), lambda b, i: (0, 0)),
                pl.BlockSpec((1, rkv), lambda b, i: (0, 0))]
    args = [x, mod[0], mod[1], w, gq.reshape(1, rq), gkv.reshape(1, rkv)]
    if rope is not None:
        in_specs += [pl.BlockSpec((tm, LANE), lambda b, i: (i, 0))] * 2
        args += [rope[0], rope[1]]
    return pl.pallas_call(
        functools.partial(_mla_down_kernel, rq=rq, rkv=rkv, rope=rope is not None),
        grid=(B, R // tm),
        in_specs=in_specs,
        out_specs=[pl.BlockSpec((1, tm, rq), lambda b, i: (b, i, 0)),
                   pl.BlockSpec((1, tm, rkv), lambda b, i: (b, i, 0)),
                   pl.BlockSpec((1, tm, LANE), lambda b, i: (b, i, 0))],
        out_shape=[jax.ShapeDtypeStruct((B, R, rq), BF16),
                   jax.ShapeDtypeStruct((B, R, rkv), BF16),
                   jax.ShapeDtypeStruct((B, R, LANE), BF16)],
        compiler_params=_params("parallel", "parallel"),
        name="mla_down",
    )(*args)


def _sink_softmax_pv(s, sk, v):
    mx = jnp.maximum(jnp.max(s, axis=1, keepdims=True), sk)
    e = jnp.exp(s - mx)
    r = 1.0 / (jnp.sum(e, axis=1, keepdims=True) + jnp.exp(sk - mx))
    return jnp.dot((e * r).astype(BF16), v, preferred_element_type=F32)


def _win_attn_kernel(sink_ref, q_ref, kp_ref, kc_ref, kn_ref, kx_ref, vp_ref, vc_ref, vn_ref, vx_ref, o_ref,
                     *, tq, n, R):
    g = pl.program_id(1)
    qi = pl.program_id(2)
    nsub = tq // WINDOW
    L = 3 * WINDOW + kx_ref.shape[1]
    row = lax.broadcasted_iota(jnp.int32, (R * WINDOW, L), 0)
    col = lax.broadcasted_iota(jnp.int32, (R * WINDOW, L), 1)
    head = lax.broadcasted_iota(jnp.int32, (R * WINDOW, 1), 0) // WINDOW
    pos = row % WINDOW
    sk = jnp.zeros((R * WINDOW, 1), F32)
    for r in range(R):
        sk = jnp.where(head == r, sink_ref[g * R + r], sk)
    scale = HEAD ** -0.5

    def piece(p_ref, c_ref, n_ref, j):
        if j < 0:
            return p_ref[0]
        if j >= nsub:
            return n_ref[0]
        return c_ref[0, j * WINDOW:(j + 1) * WINDOW]

    for j in range(nsub):
        k = jnp.concatenate([piece(kp_ref, kc_ref, kn_ref, jj) for jj in (j - 1, j, j + 1)] + [kx_ref[0]], axis=0)
        v = jnp.concatenate([piece(vp_ref, vc_ref, vn_ref, jj) for jj in (j - 1, j, j + 1)] + [vx_ref[0]], axis=0)
        base = qi * tq + j * WINDOW
        key_pos = base - WINDOW + col
        valid = ((jnp.abs(base + pos - key_pos) <= WINDOW) & (key_pos >= 0) & (key_pos < n)) | (col >= 3 * WINDOW)
        rows = slice(j * WINDOW, (j + 1) * WINDOW)
        q = jnp.concatenate([q_ref[0, rows, r * HEAD:(r + 1) * HEAD] for r in range(R)], axis=0)
        s = jnp.where(valid, _dot_nt(q, k) * scale, NEG_INF)
        o = _sink_softmax_pv(s, sk, v).astype(o_ref.dtype)
        for r in range(R):
            o_ref[0, rows, r * HEAD:(r + 1) * HEAD] = o[r * WINDOW:(r + 1) * WINDOW]


def _win_attn(qkv, qkv_ctx, sink, H, G):
    B, n, _ = qkv.shape
    m = qkv_ctx.shape[1]
    R = H // G
    assert n % WINDOW == 0
    tq = _tile(n, TW, WINDOW)
    nsub = tq // WINDOW
    nb = n // tq
    nw = n // WINDOW
    prev = lambda i: jnp.maximum(i * nsub - 1, 0)
    nxt = lambda i: jnp.minimum((i + 1) * nsub, nw - 1)
    ident = lambda i: i
    kspec = lambda f, rows, off: pl.BlockSpec((1, rows, HEAD), lambda b, g, i: (b, f(i), off + g))
    xspec = lambda off: pl.BlockSpec((1, m, HEAD), lambda b, g, i: (b, 0, off + g))
    return pl.pallas_call(
        functools.partial(_win_attn_kernel, tq=tq, n=n, R=R),
        grid=(B, G, nb),
        in_specs=[pl.BlockSpec(memory_space=pltpu.SMEM),
                  pl.BlockSpec((1, tq, R * HEAD), lambda b, g, i: (b, i, g)),
                  kspec(prev, WINDOW, H), kspec(ident, tq, H), kspec(nxt, WINDOW, H), xspec(H),
                  kspec(prev, WINDOW, H + G), kspec(ident, tq, H + G), kspec(nxt, WINDOW, H + G), xspec(H + G)],
        out_specs=pl.BlockSpec((1, tq, R * HEAD), lambda b, g, i: (b, i, g)),
        out_shape=jax.ShapeDtypeStruct((B, n, H * HEAD), BF16),
        compiler_params=_params("parallel", "parallel", "parallel"),
        name="win_attn",
    )(sink.astype(F32), qkv, qkv, qkv, qkv, qkv_ctx, qkv, qkv, qkv, qkv_ctx)


def _sink_attn_kernel(sink_ref, q_ref, k_ref, v_ref, o_ref, *, R):
    g = pl.program_id(1)
    q, k, v = q_ref[0], k_ref[0], v_ref[0]
    scale = HEAD ** -0.5
    for r in range(R):
        sl = slice(r * HEAD, (r + 1) * HEAD)
        s = _dot_nt(q[:, sl], k) * scale
        o_ref[0, :, sl] = _sink_softmax_pv(s, sink_ref[g * R + r], v).astype(o_ref.dtype)


def _sink_attn(qkv, sink, H, G):
    B, m, _ = qkv.shape
    R = H // G
    return pl.pallas_call(
        functools.partial(_sink_attn_kernel, R=R),
        grid=(B, G),
        in_specs=[pl.BlockSpec(memory_space=pltpu.SMEM),
                  pl.BlockSpec((1, m, R * HEAD), lambda b, g: (b, 0, g)),
                  pl.BlockSpec((1, m, HEAD), lambda b, g: (b, 0, H + g)),
                  pl.BlockSpec((1, m, HEAD), lambda b, g: (b, 0, H + G + g))],
        out_specs=pl.BlockSpec((1, m, R * HEAD), lambda b, g: (b, 0, g)),
        out_shape=jax.ShapeDtypeStruct((B, m, H * HEAD), BF16),
        compiler_params=_params("parallel", "parallel"),
        name="sink_attn",
    )(sink.astype(F32), qkv, qkv, qkv)


def _silu(g):
    return g * jax.nn.sigmoid(g)


def _ffn_kernel(x_ref, sh_ref, sc_ref, gate_ref, w1_ref, w3_ref, w2_ref, g_ref, b_ref, o_ref, h_ref, acc_ref,
                *, alpha, nf):
    f = pl.program_id(2)

    @pl.when(f == 0)
    def _():
        h_ref[...] = (x_ref[0] * (1.0 + sc_ref[0]) + sh_ref[0]).astype(BF16)
        acc_ref[...] = jnp.zeros(acc_ref.shape, F32)

    h = h_ref[...]
    a = _silu(jnp.dot(h, w1_ref[0], preferred_element_type=F32)) * jnp.dot(h, w3_ref[0], preferred_element_type=F32)
    acc_ref[...] += jnp.dot(a.astype(BF16), w2_ref[0], preferred_element_type=F32)

    @pl.when(f == nf - 1)
    def _():
        o_ref[0] = _res_ln(x_ref[0], acc_ref[...], gate_ref[0], g_ref[...], b_ref[...], alpha)


def _ffn(x, w13, w2, slot, mod, g, b, *, ctx, alpha):
    B, R, D = x.shape
    Fd = w2.shape[1]
    tm = _tile(R, TM)
    tf = _tile(Fd, TF, LANE)
    nf = Fd // tf
    row = _mod_row(ctx, B)
    vec = pl.BlockSpec((1, 1, D), lambda b, i, f: (row(b), 0, 0))
    return pl.pallas_call(
        functools.partial(_ffn_kernel, alpha=alpha, nf=nf),
        grid=(B, R // tm, nf),
        in_specs=[pl.BlockSpec((1, tm, D), lambda b, i, f: (b, i, 0)), vec, vec, vec,
                  pl.BlockSpec((1, D, tf), lambda b, i, f: (slot, 0, f)),
                  pl.BlockSpec((1, D, tf), lambda b, i, f: (slot, 0, nf + f)),
                  pl.BlockSpec((1, tf, D), lambda b, i, f: (slot, f, 0)),
                  pl.BlockSpec((1, D), lambda b, i, f: (0, 0)),
                  pl.BlockSpec((1, D), lambda b, i, f: (0, 0))],
        out_specs=pl.BlockSpec((1, tm, D), lambda b, i, f: (b, i, 0)),
        out_shape=jax.ShapeDtypeStruct((B, R, D), F32),
        scratch_shapes=[pltpu.VMEM((tm, D), BF16), pltpu.VMEM((tm, D), F32)],
        compiler_params=_params("parallel", "parallel", "arbitrary"),
        name="ffn",
    )(x, mod[0], mod[1], mod[2], w13, w13, w2, g.reshape(1, D), b.reshape(1, D))


def _router_kernel(x_ref, sh_ref, sc_ref, rt_ref, h_ref, idx_ref, wts_ref):
    hf = x_ref[0] * (1.0 + sc_ref[0]) + sh_ref[0]
    h_ref[0] = hf
    logits = jnp.dot(hf, rt_ref[...], preferred_element_type=F32, precision=lax.Precision.HIGHEST)
    E = logits.shape[1]
    lane = lax.broadcasted_iota(jnp.int32, logits.shape, 1)
    m1 = jnp.max(logits, axis=1, keepdims=True)
    i1 = jnp.min(jnp.where(logits == m1, lane, E), axis=1, keepdims=True)
    rest = jnp.where(lane == i1, -jnp.inf, logits)
    m2 = jnp.max(rest, axis=1, keepdims=True)
    i2 = jnp.min(jnp.where(rest == m2, lane, E), axis=1, keepdims=True)
    w1 = 1.0 / (1.0 + jnp.exp(m2 - m1))
    two = lax.broadcasted_iota(jnp.int32, idx_ref.shape[1:], 1)
    idx_ref[0] = jnp.where(two == 0, i1, i2)
    wts_ref[0] = jnp.where(two == 0, w1, 1.0 - w1)


def _router(x, router, mod, *, ctx):
    B, R, D = x.shape
    E = router.shape[1]
    tm = _tile(R, TM)
    row = _mod_row(ctx, B)
    vec = pl.BlockSpec((1, 1, D), lambda b, i: (row(b), 0, 0))
    return pl.pallas_call(
        _router_kernel,
        grid=(B, R // tm),
        in_specs=[pl.BlockSpec((1, tm, D), lambda b, i: (b, i, 0)), vec, vec,
                  pl.BlockSpec((D, E), lambda b, i: (0, 0))],
        out_specs=[pl.BlockSpec((1, tm, D), lambda b, i: (b, i, 0)),
                   pl.BlockSpec((1, tm, 2), lambda b, i: (b, i, 0)),
                   pl.BlockSpec((1, tm, 2), lambda b, i: (b, i, 0))],
        out_shape=[jax.ShapeDtypeStruct((B, R, D), F32),
                   jax.ShapeDtypeStruct((B, R, 2), jnp.int32),
                   jax.ShapeDtypeStruct((B, R, 2), F32)],
        compiler_params=_params("parallel", "parallel"),
        name="router",
    )(x, mod[0], mod[1], router)


def _route_plan(idx, wts, E, tm):
    T = idx.shape[0]
    P = 2 * T
    e_flat = idx.reshape(P)
    onehot = (e_flat[:, None] == jnp.arange(E, dtype=jnp.int32)[None, :]).astype(jnp.int32)
    counts = jnp.sum(onehot, axis=0)
    rank = jnp.take_along_axis(jnp.cumsum(onehot, axis=0) - onehot, e_flat[:, None], axis=1)[:, 0]
    padded = ((counts + tm - 1) // tm) * tm
    ends = jnp.cumsum(padded)
    starts = ends - padded
    slot = starts[e_flat] + rank
    ntiles = P // tm + E
    S = ntiles * tm
    tile_start = jnp.arange(ntiles, dtype=jnp.int32) * tm
    tile_expert = jnp.minimum(jnp.searchsorted(ends, tile_start, side="right"), E - 1).astype(jnp.int32)
    tile_rows = jnp.clip(counts[tile_expert] - (tile_start - starts[tile_expert]), 0, tm)
    tile_rows = jnp.where(tile_start < ends[E - 1], tile_rows, 0).astype(jnp.int32)
    last = jnp.maximum(jnp.sum((tile_rows > 0).astype(jnp.int32)) - 1, 0)
    tile_expert = jnp.where(tile_rows > 0, tile_expert, tile_expert[last])
    pair = jnp.zeros((S,), jnp.int32).at[slot].set(jnp.arange(P, dtype=jnp.int32))
    token, choice = pair // 2, pair % 2
    src = token
    dst = choice * T + token
    gate = wts.reshape(P)[pair]
    return tile_expert, tile_rows, src.reshape(ntiles, 1, tm), dst.reshape(ntiles, 1, tm), gate.reshape(S, 1)


def _experts_kernel(te_ref, tr_ref, src_ref, srcn_ref, dst_ref, gate_ref, h_hbm, w1_ref, w3_ref, w2_ref, y_hbm,
                    rows_ref, hb_ref, acc_ref, sem_in, sem_out, *, nf, tm, ntiles, gsteps):
    t = pl.program_id(0)
    f = pl.program_id(1)
    nrows = tr_ref[t]
    valid = nrows > 0
    slot = t % 2
    chunk = tm // gsteps

    def gather_row(idx_ref, r, s):
        return pltpu.make_async_copy(h_hbm.at[pl.ds(idx_ref[0, 0, r], 1)], rows_ref.at[s, pl.ds(r, 1)], sem_in)

    def scatter_wait(n, s):
        nfull = pl.multiple_of((n // SUBLANE) * SUBLANE, SUBLANE)

        @pl.when(nfull > 0)
        def _():
            pltpu.make_async_copy(rows_ref.at[s, pl.ds(0, nfull)], y_hbm.at[pl.ds(0, nfull)], sem_out).wait()

        def single(r, carry):
            pltpu.make_async_copy(rows_ref.at[s, pl.ds(r, 1)], y_hbm.at[pl.ds(0, 1)], sem_out).wait()
            return carry

        lax.fori_loop(nfull, n, single, 0)

    @pl.when((t == 0) & (f == 0))
    def _():
        def group(i, carry):
            for u in range(ROW_UNROLL):
                gather_row(src_ref, i * ROW_UNROLL + u, 0).start()
            return carry

        lax.fori_loop(0, tm // ROW_UNROLL, group, 0)

    @pl.when(valid & (f == 0))
    def _():
        pltpu.make_async_copy(h_hbm.at[pl.ds(0, tm)], rows_ref.at[slot], sem_in).wait()
        hb_ref[...] = rows_ref[slot].astype(BF16)
        acc_ref[...] = jnp.zeros(acc_ref.shape, F32)

    @pl.when((f == 1) & (t >= 1))
    def _():
        scatter_wait(tr_ref[jnp.maximum(t - 1, 0)], 1 - slot)

    def matmuls():
        h = hb_ref[...]
        a = _silu(jnp.dot(h, w1_ref[0], preferred_element_type=F32)) * jnp.dot(h, w3_ref[0], preferred_element_type=F32)
        acc_ref[...] += jnp.dot(a.astype(BF16), w2_ref[0], preferred_element_type=F32)

    def prefetch_chunk():
        base = (f - 1) * chunk
        for u in range(chunk):
            gather_row(srcn_ref, base + u, 1 - slot).start()

    prefetch = (tr_ref[jnp.minimum(t + 1, ntiles - 1)] > 0) & (t + 1 < ntiles) & (f >= 1) & (f <= gsteps)

    @pl.when(valid & prefetch)
    def _():
        matmuls()
        prefetch_chunk()

    @pl.when(valid & jnp.logical_not(prefetch))
    def _():
        matmuls()

    @pl.when(jnp.logical_not(valid) & prefetch)
    def _():
        prefetch_chunk()

    @pl.when(valid & (f == nf - 1))
    def _():
        rows_ref[slot] = acc_ref[...] * gate_ref[...]

        def scatter_row(r):
            return pltpu.make_async_copy(rows_ref.at[slot, pl.ds(r, 1)],
                                         y_hbm.at[pl.ds(dst_ref[0, 0, r], 1)], sem_out)

        ngroups = nrows // ROW_UNROLL

        def group(i, carry):
            for u in range(ROW_UNROLL):
                scatter_row(i * ROW_UNROLL + u).start()
            return carry

        def single(r, carry):
            scatter_row(r).start()
            return carry

        lax.fori_loop(0, ngroups, group, 0)
        lax.fori_loop(ngroups * ROW_UNROLL, nrows, single, 0)

    @pl.when(valid & (f == nf - 1) & (t == ntiles - 1))
    def _():
        scatter_wait(nrows, slot)


def _experts(h, plan, w13, w2, e_off, tm):
    T, D = h.shape
    Fd = w2.shape[1]
    tile_expert, tile_rows, src, dst, gate = plan
    tile_expert = tile_expert + e_off
    ntiles = src.shape[0]
    tf = _tile(Fd, TF, LANE)
    nf = Fd // tf
    assert nf >= 2, "the gather of the next tile is spread over the steps after the first"
    gsteps = 1
    while gsteps * 2 <= nf - 1 and tm % (gsteps * 2) == 0:
        gsteps *= 2
    fidx = lambda t, f, tv: jnp.where(tv[t] > 0, f, nf - 1)
    grid_spec = pltpu.PrefetchScalarGridSpec(
        num_scalar_prefetch=2,
        grid=(ntiles, nf),
        in_specs=[pl.BlockSpec((1, 1, tm), lambda t, f, te, tv: (t, 0, 0), memory_space=pltpu.SMEM),
                  pl.BlockSpec((1, 1, tm), lambda t, f, te, tv: (jnp.minimum(t + 1, ntiles - 1), 0, 0),
                               memory_space=pltpu.SMEM),
                  pl.BlockSpec((1, 1, tm), lambda t, f, te, tv: (t, 0, 0), memory_space=pltpu.SMEM),
                  pl.BlockSpec((tm, 1), lambda t, f, te, tv: (t, 0)),
                  pl.BlockSpec(memory_space=pl.ANY),
                  pl.BlockSpec((1, D, tf), lambda t, f, te, tv: (te[t], 0, fidx(t, f, tv))),
                  pl.BlockSpec((1, D, tf), lambda t, f, te, tv: (te[t], 0, nf + fidx(t, f, tv))),
                  pl.BlockSpec((1, tf, D), lambda t, f, te, tv: (te[t], fidx(t, f, tv), 0))],
        out_specs=pl.BlockSpec(memory_space=pl.ANY),
        scratch_shapes=[pltpu.VMEM((2, tm, D), F32), pltpu.VMEM((tm, D), BF16), pltpu.VMEM((tm, D), F32),
                        pltpu.SemaphoreType.DMA(()), pltpu.SemaphoreType.DMA(())],
    )
    return pl.pallas_call(
        functools.partial(_experts_kernel, nf=nf, tm=tm, ntiles=ntiles, gsteps=gsteps),
        grid_spec=grid_spec,
        out_shape=jax.ShapeDtypeStruct((2 * T, D), F32),
        compiler_params=pltpu.CompilerParams(dimension_semantics=("arbitrary", "arbitrary"),
                                             vmem_limit_bytes=VMEM_LIMIT),
        name="experts",
    )(tile_expert, tile_rows, src, src, dst, gate, h, w13, w13, w2)


def _combine_kernel(x_ref, y0_ref, y1_ref, gate_ref, g_ref, b_ref, o_ref, *, alpha):
    o_ref[0] = _res_ln(x_ref[0], y0_ref[...] + y1_ref[...], gate_ref[0], g_ref[...], b_ref[...], alpha)


def _combine_ln(x, y, gate, g, b, *, ctx, alpha):
    B, R, D = x.shape
    tm = _tile(R, TM)
    nb = R // tm
    row = _mod_row(ctx, B)
    return pl.pallas_call(
        functools.partial(_combine_kernel, alpha=alpha),
        grid=(B, nb),
        in_specs=[pl.BlockSpec((1, tm, D), lambda b, i: (b, i, 0)),
                  pl.BlockSpec((tm, D), lambda b, i: (b * nb + i, 0)),
                  pl.BlockSpec((tm, D), lambda b, i: (B * nb + b * nb + i, 0)),
                  pl.BlockSpec((1, 1, D), lambda b, i: (row(b), 0, 0)),
                  pl.BlockSpec((1, D), lambda b, i: (0, 0)),
                  pl.BlockSpec((1, D), lambda b, i: (0, 0))],
        out_specs=pl.BlockSpec((1, tm, D), lambda b, i: (b, i, 0)),
        out_shape=jax.ShapeDtypeStruct((B, R, D), F32),
        compiler_params=_params("parallel", "parallel"),
        name="combine_ln",
    )(x, y, y, gate, g.reshape(1, D), b.reshape(1, D))


def _moe(x, router, w13, w2, slot, mod, g, b, *, ctx, alpha):
    B, R, D = x.shape
    E = router.shape[1]
    T = B * R
    tm = _tile(2 * T, TME)
    h, idx, wts = _router(x, router, mod, ctx=ctx)
    plan = _route_plan(idx.reshape(T, 2), wts.reshape(T, 2), E, tm)
    y = _experts(h.reshape(T, D), plan, w13, w2, slot * E, tm)
    return _combine_ln(x, y, mod[2], g, b, ctx=ctx, alpha=alpha)


def _mix_diff(xl, xc, mod, need_ctx, layer, wqkv, lam, subln_g, rope128):
    D = xl.shape[2]
    H = D // (2 * HEAD)
    lam_init = 0.8 - 0.6 * math.exp(-0.3 * layer)
    w = wqkv.astype(BF16)
    qs = HEAD ** -0.5 * LOG2E
    qkv_l = _proj(xl, w, mod=mod, ctx=False, rope=rope128, rope_cols=(0, 2 * D), q_cols=D, q_scale=qs)
    qkv_c = _proj(xc, w, mod=mod, ctx=True, q_cols=D, q_scale=qs)
    o_l = _diff_attn(qkv_l, qkv_c, lam, subln_g, lam_init, H)
    o_c = _diff_attn(qkv_c, None, lam, subln_g, lam_init, H) if need_ctx else None
    return o_l, o_c


def _mix_window(xl, xc, mod, need_ctx, wqkv, sink, rope128):
    D = xl.shape[2]
    H = D // HEAD
    G = H // 4
    w = wqkv.astype(BF16)
    qkv_l = _proj(xl, w, mod=mod, ctx=False, rope=rope128, rope_cols=(0, (H + G) * HEAD))
    qkv_c = _proj(xc, w, mod=mod, ctx=True)
    o_l = _win_attn(qkv_l, qkv_c, sink, H, G)
    o_c = _sink_attn(qkv_c, sink, H, G) if need_ctx else None
    return o_l, o_c


def _mix_mla(xl, xc, mod, need_ctx, w_down, gq, gkv, w_uq, w_ukv, rope64):
    D = xl.shape[2]
    H = D // HEAD
    rq, rkv = gq.shape[0], gkv.shape[0]
    wd = jnp.pad(w_down, ((0, 0), (0, LANE - MLA_ROPE))).astype(BF16)
    wq3 = w_uq.reshape(rq, H, HEAD + MLA_ROPE)
    wq = jnp.concatenate(
        [wq3[:, :, :HEAD].reshape(rq, H * HEAD),
         jnp.pad(wq3[:, :, HEAD:], ((0, 0), (0, 0), (0, HEAD - MLA_ROPE))).reshape(rq, H * HEAD)], axis=1).astype(BF16)
    wkv3 = w_ukv.reshape(rkv, H, 2 * HEAD)
    wkv = jnp.concatenate([wkv3[:, :, :HEAD].reshape(rkv, H * HEAD),
                           wkv3[:, :, HEAD:].reshape(rkv, H * HEAD)], axis=1).astype(BF16)
    cq_l, ckv_l, kr_l = _mla_down(xl, wd, gq, gkv, mod, ctx=False, rope=rope64)
    cq_c, ckv_c, kr_c = _mla_down(xc, wd, gq, gkv, mod, ctx=True, rope=None)
    qs = (HEAD + MLA_ROPE) ** -0.5 * LOG2E
    q_l = _proj(cq_l, wq, rope=rope64, rope_cols=(H * HEAD, 2 * H * HEAD), q_cols=2 * H * HEAD, q_scale=qs)
    kv_l = _proj(ckv_l, wkv)
    kv_c = _proj(ckv_c, wkv)
    o_l = _mla_attn(q_l, kv_l, kr_l, kv_c, kr_c, H)
    o_c = None
    if need_ctx:
        q_c = _proj(cq_c, wq, q_cols=2 * H * HEAD, q_scale=qs)
        o_c = _mla_attn(q_c, kv_c, kr_c, None, None, H)
    return o_l, o_c


def kernel(x, c, ctx, c_ctx, mod_w, mod_b, ln1_g, ln1_b, ln2_g, ln2_b, da_wqkv, da_lambda, da_subln_g, da_wo, wa_wqkv, wa_sink, wa_wo, mla_w_down, mla_q_norm_g, mla_kv_norm_g, mla_w_uq, mla_w_ukv, mla_wo, ffn_w13, ffn_w2, moe_router, moe_w13, moe_w2):
    B, n, D = x.shape
    depth = mod_w.shape[0]
    alpha = (2.0 * depth) ** 0.25
    rows = SUBLANE * (-(-(B + 1) // SUBLANE))
    cc = jnp.concatenate([c, c_ctx[None, :], jnp.zeros((rows - B - 1, D), F32)], axis=0)
    mod_all = _modulation(cc, mod_w, mod_b)
    rope128 = _rope_tables(n, HEAD)
    rope64 = _rope_tables(n, MLA_ROPE)
    ffn_w13_b, ffn_w2_b = ffn_w13.astype(BF16), ffn_w2.astype(BF16)
    moe_w13_b = moe_w13.astype(BF16).reshape((-1,) + moe_w13.shape[2:])
    moe_w2_b = moe_w2.astype(BF16).reshape((-1,) + moe_w2.shape[2:])
    xl, xc = x, ctx
    for i in range(depth):
        need_ctx = i < depth - 1
        mod = [mod_all[i, :, k * D:(k + 1) * D].reshape(rows, 1, D) for k in range(6)]
        kind, slot = i % N_MIXERS, i // N_MIXERS
        if kind == 0:
            o_l, o_c = _mix_diff(xl, xc, mod[0:2], need_ctx, i, da_wqkv[slot], da_lambda[slot],
                                 da_subln_g[slot], rope128)
            wo = da_wo[slot]
        elif kind == 1:
            o_l, o_c = _mix_window(xl, xc, mod[0:2], need_ctx, wa_wqkv[slot], wa_sink[slot], rope128)
            wo = wa_wo[slot]
        else:
            o_l, o_c = _mix_mla(xl, xc, mod[0:2], need_ctx, mla_w_down[slot], mla_q_norm_g[slot],
                                mla_kv_norm_g[slot], mla_w_uq[slot], mla_w_ukv[slot], rope64)
            wo = mla_wo[slot]
        wo = wo.astype(BF16)
        xl = _oproj_ln(xl, o_l, wo, mod[2], ln1_g[i], ln1_b[i], ctx=False, alpha=alpha)
        if need_ctx:
            xc = _oproj_ln(xc, o_c, wo, mod[2], ln1_g[i], ln1_b[i], ctx=True, alpha=alpha)
        cslot = i // 2
        if i % 2 == 0:
            xl = _ffn(xl, ffn_w13_b, ffn_w2_b, cslot, mod[3:6], ln2_g[i], ln2_b[i], ctx=False, alpha=alpha)
            if need_ctx:
                xc = _ffn(xc, ffn_w13_b, ffn_w2_b, cslot, mod[3:6], ln2_g[i], ln2_b[i], ctx=True, alpha=alpha)
        else:
            xl = _moe(xl, moe_router[cslot], moe_w13_b, moe_w2_b, cslot, mod[3:6], ln2_g[i], ln2_b[i],
                      ctx=False, alpha=alpha)
            if need_ctx:
                xc = _moe(xc, moe_router[cslot], moe_w13_b, moe_w2_b, cslot, mod[3:6], ln2_g[i], ln2_b[i],
                          ctx=True, alpha=alpha)
    return xl
```

```python
import functools
import math

import jax
import jax.numpy as jnp
from jax import lax
from jax.experimental import pallas as pl
from jax.experimental.pallas import tpu as pltpu

F32 = jnp.float32
BF16 = jnp.bfloat16

LANE = 128
SUBLANE = 8
VMEM_LIMIT = 56 * 1024 * 1024

GRID_W = 64
ROPE_BASE = 10000.0
LN_EPS = 1e-5
RMS_EPS = 1e-6
NEG_INF = -1e30
HEAD = 128
WINDOW = 128
MLA_ROPE = 64
N_MIXERS = 3
LOG2E = math.log2(math.e)

TM = 512
TMP = 1024
TW = 512
TN = 1024
TF = 512
TQ = 2048
TK = 2816
RB = 256
TME = 1024
ROW_UNROLL = 8


def _tile(dim, pref, align=SUBLANE):
    if dim <= pref:
        return dim
    t = (pref // align) * align
    while t >= align:
        if dim % t == 0:
            return t
        t -= align
    return dim


def _params(*sem):
    return pltpu.CompilerParams(dimension_semantics=sem, vmem_limit_bytes=VMEM_LIMIT)


def _mod_row(ctx, nb):
    return (lambda b: nb) if ctx else (lambda b: b)


def _mod_kernel(s_ref, w_ref, b_ref, o_ref):
    s = s_ref[...]
    s = s * jax.nn.sigmoid(s)
    o_ref[0] = jnp.dot(s.astype(BF16), w_ref[0].astype(BF16), preferred_element_type=F32) + b_ref[0]


def _modulation(cc, mod_w, mod_b):
    depth, D, N = mod_w.shape
    R = cc.shape[0]
    tn = _tile(N, 1024, LANE)
    return pl.pallas_call(
        _mod_kernel,
        grid=(depth, N // tn),
        in_specs=[pl.BlockSpec((R, D), lambda l, j: (0, 0)),
                  pl.BlockSpec((1, D, tn), lambda l, j: (l, 0, j)),
                  pl.BlockSpec((1, 1, tn), lambda l, j: (l, 0, j))],
        out_specs=pl.BlockSpec((1, R, tn), lambda l, j: (l, 0, j)),
        out_shape=jax.ShapeDtypeStruct((depth, R, N), F32),
        compiler_params=_params("parallel", "parallel"),
        name="modulation",
    )(cc, mod_w, mod_b.reshape(depth, 1, N))


def _rope_tables(n, dim):
    half = dim // 4
    t = jnp.arange(n)
    row = (t // GRID_W).astype(F32)
    col = (t % GRID_W).astype(F32)
    inv = ROPE_BASE ** (-jnp.arange(half, dtype=F32) / half)
    ar = row[:, None] * inv[None, :]
    ac = col[:, None] * inv[None, :]
    cos = jnp.concatenate([jnp.cos(ar), jnp.cos(ar), jnp.cos(ac), jnp.cos(ac)], axis=1)
    sin = jnp.concatenate([-jnp.sin(ar), jnp.sin(ar), -jnp.sin(ac), jnp.sin(ac)], axis=1)
    rep = LANE // dim
    return jnp.tile(cos, (1, rep)), jnp.tile(sin, (1, rep)), half


def _rope_chunk(yc, cos, sin, half):
    lane = lax.broadcasted_iota(jnp.int32, (1, LANE), 1)
    first = (lane % (2 * half)) < half
    rot = jnp.where(first, pltpu.roll(yc, LANE - half, 1), pltpu.roll(yc, half, 1))
    return yc * cos + rot * sin


def _proj_kernel(*refs, modulate, rope_lo, rope_hi, half, q_hi, q_scale):
    refs = list(refs)
    x_ref = refs.pop(0)
    if modulate:
        sh_ref, sc_ref = refs.pop(0), refs.pop(0)
    w_ref = refs.pop(0)
    if rope_hi > rope_lo:
        cos_ref, sin_ref = refs.pop(0), refs.pop(0)
    o_ref = refs.pop(0)
    j = pl.program_id(2)
    if modulate:
        h_ref = refs.pop(0)

        @pl.when(j == 0)
        def _():
            h_ref[...] = (x_ref[0] * (1.0 + sc_ref[0]) + sh_ref[0]).astype(BF16)

        h = h_ref[...]
    else:
        h = x_ref[0]
    y = jnp.dot(h, w_ref[...], preferred_element_type=F32)
    if q_hi > 0:
        y = y * jnp.where(j < q_hi, q_scale, 1.0)
    if rope_hi > rope_lo:
        in_rope = (j >= rope_lo) & (j < rope_hi)

        @pl.when(in_rope)
        def _():
            cos, sin = cos_ref[...], sin_ref[...]
            for cidx in range(y.shape[1] // LANE):
                sl = slice(cidx * LANE, (cidx + 1) * LANE)
                o_ref[0, :, sl] = _rope_chunk(y[:, sl], cos, sin, half).astype(o_ref.dtype)

        @pl.when(jnp.logical_not(in_rope))
        def _():
            o_ref[0] = y.astype(o_ref.dtype)
    else:
        o_ref[0] = y.astype(o_ref.dtype)


def _proj(x, w, *, mod=None, ctx=False, rope=None, rope_cols=(0, 0), q_cols=0, q_scale=1.0):
    B, R, K = x.shape
    N = w.shape[1]
    tm = _tile(R, TMP)
    g = math.gcd(N, q_cols)
    if rope is not None:
        g = math.gcd(g, math.gcd(rope_cols[0], rope_cols[1]))
    tn = _tile(g, TN, LANE)
    modulate = mod is not None
    args = [x]
    in_specs = [pl.BlockSpec((1, tm, K), lambda b, i, j: (b, i, 0))]
    if modulate:
        shift, scale = mod
        row = _mod_row(ctx, B)
        in_specs += [pl.BlockSpec((1, 1, K), lambda b, i, j: (row(b), 0, 0))] * 2
        args += [shift, scale]
    in_specs.append(pl.BlockSpec((K, tn), lambda b, i, j: (0, j)))
    args.append(w)
    half = 0
    if rope is not None:
        cos, sin, half = rope
        in_specs += [pl.BlockSpec((tm, LANE), lambda b, i, j: (i, 0))] * 2
        args += [cos, sin]
    kern = functools.partial(_proj_kernel, modulate=modulate, rope_lo=rope_cols[0] // tn,
                             rope_hi=rope_cols[1] // tn if rope is not None else 0, half=half,
                             q_hi=q_cols // tn, q_scale=q_scale)
    return pl.pallas_call(
        kern,
        grid=(B, R // tm, N // tn),
        in_specs=in_specs,
        out_specs=pl.BlockSpec((1, tm, tn), lambda b, i, j: (b, i, j)),
        out_shape=jax.ShapeDtypeStruct((B, R, N), BF16),
        scratch_shapes=[pltpu.VMEM((tm, K), BF16)] if modulate else [],
        compiler_params=_params("parallel", "parallel", "arbitrary"),
        name="proj",
    )(*args)


def _res_ln(x, y, gate, g, b, alpha):
    z = alpha * x + gate * y
    mu = jnp.mean(z, axis=-1, keepdims=True)
    zc = z - mu
    var = jnp.mean(zc * zc, axis=-1, keepdims=True)
    return zc * lax.rsqrt(var + LN_EPS) * g + b


def _oproj_kernel(x_ref, o_ref, w_ref, gate_ref, g_ref, b_ref, out_ref, *, alpha):
    y = jnp.dot(o_ref[0], w_ref[...], preferred_element_type=F32)
    out_ref[0] = _res_ln(x_ref[0], y, gate_ref[0], g_ref[...], b_ref[...], alpha)


def _oproj_ln(x, o, w, gate, g, b, *, ctx, alpha):
    B, R, D = x.shape
    K = o.shape[2]
    tm = _tile(R, TM)
    row = _mod_row(ctx, B)
    return pl.pallas_call(
        functools.partial(_oproj_kernel, alpha=alpha),
        grid=(B, R // tm),
        in_specs=[pl.BlockSpec((1, tm, D), lambda b, i: (b, i, 0)),
                  pl.BlockSpec((1, tm, K), lambda b, i: (b, i, 0)),
                  pl.BlockSpec((K, D), lambda b, i: (0, 0)),
                  pl.BlockSpec((1, 1, D), lambda b, i: (row(b), 0, 0)),
                  pl.BlockSpec((1, D), lambda b, i: (0, 0)),
                  pl.BlockSpec((1, D), lambda b, i: (0, 0))],
        out_specs=pl.BlockSpec((1, tm, D), lambda b, i: (b, i, 0)),
        out_shape=jax.ShapeDtypeStruct((B, R, D), F32),
        compiler_params=_params("parallel", "parallel"),
        name="oproj_ln",
    )(x, o, w, gate, g.reshape(1, D), b.reshape(1, D))


def _flash_init(m_ref, l_ref, acc_ref):
    m_ref[...] = jnp.full(m_ref.shape, NEG_INF, F32)
    l_ref[...] = jnp.zeros(l_ref.shape, F32)
    acc_ref[...] = jnp.zeros(acc_ref.shape, F32)


def _dot_nt(a, b):
    return lax.dot_general(a, b, (((1,), (1,)), ((), ())), preferred_element_type=F32)


def _flash_step(q_fn, k_fn, v_fn, m_ref, l_ref, acc_ref, *, chains, tq):
    rb = min(RB, tq)

    def body(r, carry):
        rows = pl.ds(pl.multiple_of(r * rb, rb), rb)
        for c in range(chains):
            s = _dot_nt(q_fn(c, rows), k_fn(c))
            m_prev = m_ref[c, rows]
            m_new = jnp.maximum(m_prev, jnp.max(s, axis=1, keepdims=True))
            a = jnp.exp2(m_prev - m_new)
            p = jnp.exp2(s - m_new)
            l_ref[c, rows] = a * l_ref[c, rows] + jnp.sum(p, axis=1, keepdims=True)
            acc_ref[c, rows] = a * acc_ref[c, rows] + jnp.dot(p.astype(BF16), v_fn(c), preferred_element_type=F32)
            m_ref[c, rows] = m_new
        return carry

    lax.fori_loop(0, tq // rb, body, 0, unroll=True)


def _stage_keys(ki, nk, tail, pairs):
    for lat_ref, ctx_ref, buf_ref in pairs:
        if tail is None:
            buf_ref[...] = lat_ref[0]
            continue

        @pl.when(ki < nk - 1)
        def _():
            buf_ref[...] = lat_ref[0]

        @pl.when(ki == nk - 1)
        def _():
            buf_ref[:tail] = lat_ref[0, :tail]
            buf_ref[tail:] = ctx_ref[0]


def _diff_attn_kernel(lam_ref, g_ref, q_ref, k_ref, v_ref, *rest, lam_init, nk, tail):
    if tail is None:
        kc_ref = vc_ref = None
        o_ref, m_ref, l_ref, acc_ref, kbuf, vbuf = rest
    else:
        kc_ref, vc_ref, o_ref, m_ref, l_ref, acc_ref, kbuf, vbuf = rest
    ki = pl.program_id(3)

    @pl.when(ki == 0)
    def _():
        _flash_init(m_ref, l_ref, acc_ref)

    _stage_keys(ki, nk, tail, [(k_ref, kc_ref, kbuf), (v_ref, vc_ref, vbuf)])
    _flash_step(lambda c, rows: q_ref[0, rows, c * HEAD:(c + 1) * HEAD],
                lambda c: kbuf[:, c * HEAD:(c + 1) * HEAD],
                lambda c: vbuf[...],
                m_ref, l_ref, acc_ref, chains=2, tq=q_ref.shape[1])

    @pl.when(ki == nk - 1)
    def _():
        lam = lam_ref[...]
        lam_full = (jnp.exp(jnp.sum(lam[0:1] * lam[1:2], axis=1, keepdims=True))
                    - jnp.exp(jnp.sum(lam[2:3] * lam[3:4], axis=1, keepdims=True)) + lam_init)
        o = acc_ref[0] / l_ref[0] - lam_full * (acc_ref[1] / l_ref[1])
        ms = jnp.mean(o * o, axis=1, keepdims=True)
        o_ref[0] = (o * lax.rsqrt(ms + RMS_EPS) * g_ref[...] * (1.0 - lam_init)).astype(o_ref.dtype)


def _kv_tiling(n, m):
    tk = _tile(n + m, TK)
    nk = (n + m) // tk
    if m == 0:
        return tk, nk, None
    tail = n - (nk - 1) * tk
    assert 0 < tail and tail + m == tk and tail % 16 == 0, "context keys must complete the last kv block"
    return tk, nk, tail


def _diff_attn(qkv, qkv_ctx, lam, subln_g, lam_init, H):
    B, nq, _ = qkv.shape
    m = 0 if qkv_ctx is None else qkv_ctx.shape[1]
    W = 2 * HEAD
    tq = _tile(nq, TQ)
    tk, nk, tail = _kv_tiling(nq, m)
    in_specs = [pl.BlockSpec((4, HEAD), lambda b, h, i, k: (0, 0)),
                pl.BlockSpec((1, W), lambda b, h, i, k: (0, 0)),
                pl.BlockSpec((1, tq, W), lambda b, h, i, k: (b, i, h)),
                pl.BlockSpec((1, tk, W), lambda b, h, i, k: (b, k, H + h)),
                pl.BlockSpec((1, tk, W), lambda b, h, i, k: (b, k, 2 * H + h))]
    args = [lam.astype(F32), subln_g.reshape(1, W), qkv, qkv, qkv]
    if m:
        in_specs += [pl.BlockSpec((1, m, W), lambda b, h, i, k: (b, 0, H + h)),
                     pl.BlockSpec((1, m, W), lambda b, h, i, k: (b, 0, 2 * H + h))]
        args += [qkv_ctx, qkv_ctx]
    return pl.pallas_call(
        functools.partial(_diff_attn_kernel, lam_init=lam_init, nk=nk, tail=tail),
        grid=(B, H, nq // tq, nk),
        in_specs=in_specs,
        out_specs=pl.BlockSpec((1, tq, W), lambda b, h, i, k: (b, i, h)),
        out_shape=jax.ShapeDtypeStruct((B, nq, H * W), BF16),
        scratch_shapes=[pltpu.VMEM((2, tq, 1), F32), pltpu.VMEM((2, tq, 1), F32),
                        pltpu.VMEM((2, tq, W), F32), pltpu.VMEM((tk, W), BF16), pltpu.VMEM((tk, W), BF16)],
        compiler_params=_params("parallel", "parallel", "parallel", "arbitrary"),
        name="diff_attn",
    )(*args)


def _mla_attn_kernel(qn_ref, qr_ref, kn_ref, kr_ref, v_ref, *rest, nk, tail):
    if tail is None:
        knc_ref = krc_ref = vc_ref = None
        o_ref, m_ref, l_ref, acc_ref, k_ref, knbuf, krbuf, vbuf = rest
    else:
        knc_ref, krc_ref, vc_ref, o_ref, m_ref, l_ref, acc_ref, k_ref, knbuf, krbuf, vbuf = rest
    ki = pl.program_id(3)

    @pl.when(ki == 0)
    def _():
        _flash_init(m_ref, l_ref, acc_ref)

    _stage_keys(ki, nk, tail, [(kn_ref, knc_ref, knbuf), (kr_ref, krc_ref, krbuf), (v_ref, vc_ref, vbuf)])
    for c in range(2):
        k_ref[c, :, :HEAD] = knbuf[:, c * HEAD:(c + 1) * HEAD]
        k_ref[c, :, HEAD:] = krbuf[...]

    def q_fn(c, rows):
        sl = slice(c * HEAD, (c + 1) * HEAD)
        return jnp.concatenate([qn_ref[0, rows, sl], qr_ref[0, rows, sl]], axis=1)

    _flash_step(q_fn, lambda c: k_ref[c], lambda c: vbuf[:, c * HEAD:(c + 1) * HEAD],
                m_ref, l_ref, acc_ref, chains=2, tq=qn_ref.shape[1])

    @pl.when(ki == nk - 1)
    def _():
        for c in range(2):
            o_ref[0, :, c * HEAD:(c + 1) * HEAD] = (acc_ref[c] / l_ref[c]).astype(o_ref.dtype)


def _mla_attn(q_src, kv, kr, kv_ctx, kr_ctx, H):
    B, nq, _ = q_src.shape
    m = 0 if kv_ctx is None else kv_ctx.shape[1]
    assert H % 2 == 0
    P = H // 2
    W = 2 * HEAD
    tq = _tile(nq, TQ)
    tk, nk, tail = _kv_tiling(nq, m)
    in_specs = [pl.BlockSpec((1, tq, W), lambda b, h, i, k: (b, i, h)),
                pl.BlockSpec((1, tq, W), lambda b, h, i, k: (b, i, P + h)),
                pl.BlockSpec((1, tk, W), lambda b, h, i, k: (b, k, h)),
                pl.BlockSpec((1, tk, HEAD), lambda b, h, i, k: (b, k, 0)),
                pl.BlockSpec((1, tk, W), lambda b, h, i, k: (b, k, P + h))]
    args = [q_src, q_src, kv, kr, kv]
    if m:
        in_specs += [pl.BlockSpec((1, m, W), lambda b, h, i, k: (b, 0, h)),
                     pl.BlockSpec((1, m, HEAD), lambda b, h, i, k: (b, 0, 0)),
                     pl.BlockSpec((1, m, W), lambda b, h, i, k: (b, 0, P + h))]
        args += [kv_ctx, kr_ctx, kv_ctx]
    return pl.pallas_call(
        functools.partial(_mla_attn_kernel, nk=nk, tail=tail),
        grid=(B, P, nq // tq, nk),
        in_specs=in_specs,
        out_specs=pl.BlockSpec((1, tq, W), lambda b, h, i, k: (b, i, h)),
        out_shape=jax.ShapeDtypeStruct((B, nq, H * HEAD), BF16),
        scratch_shapes=[pltpu.VMEM((2, tq, 1), F32), pltpu.VMEM((2, tq, 1), F32),
                        pltpu.VMEM((2, tq, HEAD), F32), pltpu.VMEM((2, tk, W), BF16),
                        pltpu.VMEM((tk, W), BF16), pltpu.VMEM((tk, HEAD), BF16), pltpu.VMEM((tk, W), BF16)],
        compiler_params=_params("parallel", "parallel", "parallel", "arbitrary"),
        name="mla_attn",
    )(*args)


def _mla_down_kernel(x_ref, sh_ref, sc_ref, w_ref, gq_ref, gkv_ref, *rest, rq, rkv, rope):
    if rope:
        cos_ref, sin_ref, cq_ref, ckv_ref, kr_ref = rest
    else:
        cq_ref, ckv_ref, kr_ref = rest
    h = (x_ref[0] * (1.0 + sc_ref[0]) + sh_ref[0]).astype(BF16)
    y = jnp.dot(h, w_ref[...], preferred_element_type=F32)

    def rms(t, g):
        return t * lax.rsqrt(jnp.mean(t * t, axis=1, keepdims=True) + RMS_EPS) * g

    cq_ref[0] = rms(y[:, :rq], gq_ref[...]).astype(BF16)
    ckv_ref[0] = rms(y[:, rq:rq + rkv], gkv_ref[...]).astype(BF16)
    kr = y[:, rq + rkv:]
    if rope:
        kr = _rope_chunk(kr, cos_ref[...], sin_ref[...], MLA_ROPE // 4)
    kr_ref[0] = kr.astype(BF16)


def _mla_down(x, w, gq, gkv, mod, *, ctx, rope):
    B, R, D = x.shape
    rq, rkv = gq.shape[0], gkv.shape[0]
    N = w.shape[1]
    tm = _tile(R, TM)
    row = _mod_row(ctx, B)
    in_specs = [pl.BlockSpec((1, tm, D), lambda b, i: (b, i, 0)),
                pl.BlockSpec((1, 1, D), lambda b, i: (row(b), 0, 0)),
                pl.BlockSpec((1, 1, D), lambda b, i: (row(b), 0, 0)),
                pl.BlockSpec((D, N), lambda b, i: (0, 0)),
                pl.BlockSpec((1, rq), lambda b, i: (0, 0)),
                pl.BlockSpec((1, rkv), lambda b, i: (0, 0))]
    args = [x, mod[0], mod[1], w, gq.reshape(1, rq), gkv.reshape(1, rkv)]
    if rope is not None:
        in_specs += [pl.BlockSpec((tm, LANE), lambda b, i: (i, 0))] * 2
        args += [rope[0], rope[1]]
    return pl.pallas_call(
        functools.partial(_mla_down_kernel, rq=rq, rkv=rkv, rope=rope is not None),
        grid=(B, R // tm),
        in_specs=in_specs,
        out_specs=[pl.BlockSpec((1, tm, rq), lambda b, i: (b, i, 0)),
                   pl.BlockSpec((1, tm, rkv), lambda b, i: (b, i, 0)),
                   pl.BlockSpec((1, tm, LANE), lambda b, i: (b, i, 0))],
        out_shape=[jax.ShapeDtypeStruct((B, R, rq), BF16),
                   jax.ShapeDtypeStruct((B, R, rkv), BF16),
                   jax.ShapeDtypeStruct((B, R, LANE), BF16)],
        compiler_params=_params("parallel", "parallel"),
        name="mla_down",
    )(*args)


def _sink_softmax_pv(s, sk, v):
    mx = jnp.maximum(jnp.max(s, axis=1, keepdims=True), sk)
    e = jnp.exp(s - mx)
    r = 1.0 / (jnp.sum(e, axis=1, keepdims=True) + jnp.exp(sk - mx))
    return jnp.dot((e * r).astype(BF16), v, preferred_element_type=F32)


def _win_attn_kernel(sink_ref, q_ref, kp_ref, kc_ref, kn_ref, kx_ref, vp_ref, vc_ref, vn_ref, vx_ref, o_ref,
                     *, tq, n, R):
    g = pl.program_id(1)
    qi = pl.program_id(2)
    nsub = tq // WINDOW
    L = 3 * WINDOW + kx_ref.shape[1]
    row = lax.broadcasted_iota(jnp.int32, (R * WINDOW, L), 0)
    col = lax.broadcasted_iota(jnp.int32, (R * WINDOW, L), 1)
    head = lax.broadcasted_iota(jnp.int32, (R * WINDOW, 1), 0) // WINDOW
    pos = row % WINDOW
    sk = jnp.zeros((R * WINDOW, 1), F32)
    for r in range(R):
        sk = jnp.where(head == r, sink_ref[g * R + r], sk)
    scale = HEAD ** -0.5

    def piece(p_ref, c_ref, n_ref, j):
        if j < 0:
            return p_ref[0]
        if j >= nsub:
            return n_ref[0]
        return c_ref[0, j * WINDOW:(j + 1) * WINDOW]

    for j in range(nsub):
        k = jnp.concatenate([piece(kp_ref, kc_ref, kn_ref, jj) for jj in (j - 1, j, j + 1)] + [kx_ref[0]], axis=0)
        v = jnp.concatenate([piece(vp_ref, vc_ref, vn_ref, jj) for jj in (j - 1, j, j + 1)] + [vx_ref[0]], axis=0)
        base = qi * tq + j * WINDOW
        key_pos = base - WINDOW + col
        valid = ((jnp.abs(base + pos - key_pos) <= WINDOW) & (key_pos >= 0) & (key_pos < n)) | (col >= 3 * WINDOW)
        rows = slice(j * WINDOW, (j + 1) * WINDOW)
        q = jnp.concatenate([q_ref[0, rows, r * HEAD:(r + 1) * HEAD] for r in range(R)], axis=0)
        s = jnp.where(valid, _dot_nt(q, k) * scale, NEG_INF)
        o = _sink_softmax_pv(s, sk, v).astype(o_ref.dtype)
        for r in range(R):
            o_ref[0, rows, r * HEAD:(r + 1) * HEAD] = o[r * WINDOW:(r + 1) * WINDOW]


def _win_attn(qkv, qkv_ctx, sink, H, G):
    B, n, _ = qkv.shape
    m = qkv_ctx.shape[1]
    R = H // G
    assert n % WINDOW == 0
    tq = _tile(n, TW, WINDOW)
    nsub = tq // WINDOW
    nb = n // tq
    nw = n // WINDOW
    prev = lambda i: jnp.maximum(i * nsub - 1, 0)
    nxt = lambda i: jnp.minimum((i + 1) * nsub, nw - 1)
    ident = lambda i: i
    kspec = lambda f, rows, off: pl.BlockSpec((1, rows, HEAD), lambda b, g, i: (b, f(i), off + g))
    xspec = lambda off: pl.BlockSpec((1, m, HEAD), lambda b, g, i: (b, 0, off + g))
    return pl.pallas_call(
        functools.partial(_win_attn_kernel, tq=tq, n=n, R=R),
        grid=(B, G, nb),
        in_specs=[pl.BlockSpec(memory_space=pltpu.SMEM),
                  pl.BlockSpec((1, tq, R * HEAD), lambda b, g, i: (b, i, g)),
                  kspec(prev, WINDOW, H), kspec(ident, tq, H), kspec(nxt, WINDOW, H), xspec(H),
                  kspec(prev, WINDOW, H + G), kspec(ident, tq, H + G), kspec(nxt, WINDOW, H + G), xspec(H + G)],
        out_specs=pl.BlockSpec((1, tq, R * HEAD), lambda b, g, i: (b, i, g)),
        out_shape=jax.ShapeDtypeStruct((B, n, H * HEAD), BF16),
        compiler_params=_params("parallel", "parallel", "parallel"),
        name="win_attn",
    )(sink.astype(F32), qkv, qkv, qkv, qkv, qkv_ctx, qkv, qkv, qkv, qkv_ctx)


def _sink_attn_kernel(sink_ref, q_ref, k_ref, v_ref, o_ref, *, R):
    g = pl.program_id(1)
    q, k, v = q_ref[0], k_ref[0], v_ref[0]
    scale = HEAD ** -0.5
    for r in range(R):
        sl = slice(r * HEAD, (r + 1) * HEAD)
        s = _dot_nt(q[:, sl], k) * scale
        o_ref[0, :, sl] = _sink_softmax_pv(s, sink_ref[g * R + r], v).astype(o_ref.dtype)


def _sink_attn(qkv, sink, H, G):
    B, m, _ = qkv.shape
    R = H // G
    return pl.pallas_call(
        functools.partial(_sink_attn_kernel, R=R),
        grid=(B, G),
        in_specs=[pl.BlockSpec(memory_space=pltpu.SMEM),
                  pl.BlockSpec((1, m, R * HEAD), lambda b, g: (b, 0, g)),
                  pl.BlockSpec((1, m, HEAD), lambda b, g: (b, 0, H + g)),
                  pl.BlockSpec((1, m, HEAD), lambda b, g: (b, 0, H + G + g))],
        out_specs=pl.BlockSpec((1, m, R * HEAD), lambda b, g: (b, 0, g)),
        out_shape=jax.ShapeDtypeStruct((B, m, H * HEAD), BF16),
        compiler_params=_params("parallel", "parallel"),
        name="sink_attn",
    )(sink.astype(F32), qkv, qkv, qkv)


def _silu(g):
    return g * jax.nn.sigmoid(g)


def _ffn_kernel(x_ref, sh_ref, sc_ref, gate_ref, w1_ref, w3_ref, w2_ref, g_ref, b_ref, o_ref, h_ref, acc_ref,
                *, alpha, nf):
    f = pl.program_id(2)

    @pl.when(f == 0)
    def _():
        h_ref[...] = (x_ref[0] * (1.0 + sc_ref[0]) + sh_ref[0]).astype(BF16)
        acc_ref[...] = jnp.zeros(acc_ref.shape, F32)

    h = h_ref[...]
    a = _silu(jnp.dot(h, w1_ref[0], preferred_element_type=F32)) * jnp.dot(h, w3_ref[0], preferred_element_type=F32)
    acc_ref[...] += jnp.dot(a.astype(BF16), w2_ref[0], preferred_element_type=F32)

    @pl.when(f == nf - 1)
    def _():
        o_ref[0] = _res_ln(x_ref[0], acc_ref[...], gate_ref[0], g_ref[...], b_ref[...], alpha)


def _ffn(x, w13, w2, slot, mod, g, b, *, ctx, alpha):
    B, R, D = x.shape
    Fd = w2.shape[1]
    tm = _tile(R, TM)
    tf = _tile(Fd, TF, LANE)
    nf = Fd // tf
    row = _mod_row(ctx, B)
    vec = pl.BlockSpec((1, 1, D), lambda b, i, f: (row(b), 0, 0))
    return pl.pallas_call(
        functools.partial(_ffn_kernel, alpha=alpha, nf=nf),
        grid=(B, R // tm, nf),
        in_specs=[pl.BlockSpec((1, tm, D), lambda b, i, f: (b, i, 0)), vec, vec, vec,
                  pl.BlockSpec((1, D, tf), lambda b, i, f: (slot, 0, f)),
                  pl.BlockSpec((1, D, tf), lambda b, i, f: (slot, 0, nf + f)),
                  pl.BlockSpec((1, tf, D), lambda b, i, f: (slot, f, 0)),
                  pl.BlockSpec((1, D), lambda b, i, f: (0, 0)),
                  pl.BlockSpec((1, D), lambda b, i, f: (0, 0))],
        out_specs=pl.BlockSpec((1, tm, D), lambda b, i, f: (b, i, 0)),
        out_shape=jax.ShapeDtypeStruct((B, R, D), F32),
        scratch_shapes=[pltpu.VMEM((tm, D), BF16), pltpu.VMEM((tm, D), F32)],
        compiler_params=_params("parallel", "parallel", "arbitrary"),
        name="ffn",
    )(x, mod[0], mod[1], mod[2], w13, w13, w2, g.reshape(1, D), b.reshape(1, D))


def _router_kernel(x_ref, sh_ref, sc_ref, rt_ref, h_ref, idx_ref, wts_ref):
    hf = x_ref[0] * (1.0 + sc_ref[0]) + sh_ref[0]
    h_ref[0] = hf
    logits = jnp.dot(hf, rt_ref[...], preferred_element_type=F32, precision=lax.Precision.HIGHEST)
    E = logits.shape[1]
    lane = lax.broadcasted_iota(jnp.int32, logits.shape, 1)
    m1 = jnp.max(logits, axis=1, keepdims=True)
    i1 = jnp.min(jnp.where(logits == m1, lane, E), axis=1, keepdims=True)
    rest = jnp.where(lane == i1, -jnp.inf, logits)
    m2 = jnp.max(rest, axis=1, keepdims=True)
    i2 = jnp.min(jnp.where(rest == m2, lane, E), axis=1, keepdims=True)
    w1 = 1.0 / (1.0 + jnp.exp(m2 - m1))
    two = lax.broadcasted_iota(jnp.int32, idx_ref.shape[1:], 1)
    idx_ref[0] = jnp.where(two == 0, i1, i2)
    wts_ref[0] = jnp.where(two == 0, w1, 1.0 - w1)


def _router(x, router, mod, *, ctx):
    B, R, D = x.shape
    E = router.shape[1]
    tm = _tile(R, TM)
    row = _mod_row(ctx, B)
    vec = pl.BlockSpec((1, 1, D), lambda b, i: (row(b), 0, 0))
    return pl.pallas_call(
        _router_kernel,
        grid=(B, R // tm),
        in_specs=[pl.BlockSpec((1, tm, D), lambda b, i: (b, i, 0)), vec, vec,
                  pl.BlockSpec((D, E), lambda b, i: (0, 0))],
        out_specs=[pl.BlockSpec((1, tm, D), lambda b, i: (b, i, 0)),
                   pl.BlockSpec((1, tm, 2), lambda b, i: (b, i, 0)),
                   pl.BlockSpec((1, tm, 2), lambda b, i: (b, i, 0))],
        out_shape=[jax.ShapeDtypeStruct((B, R, D), F32),
                   jax.ShapeDtypeStruct((B, R, 2), jnp.int32),
                   jax.ShapeDtypeStruct((B, R, 2), F32)],
        compiler_params=_params("parallel", "parallel"),
        name="router",
    )(x, mod[0], mod[1], router)


def _route_plan(idx, wts, E, tm):
    T = idx.shape[0]
    P = 2 * T
    e_flat = idx.reshape(P)
    onehot = (e_flat[:, None] == jnp.arange(E, dtype=jnp.int32)[None, :]).astype(jnp.int32)
    counts = jnp.sum(onehot, axis=0)
    rank = jnp.take_along_axis(jnp.cumsum(onehot, axis=0) - onehot, e_flat[:, None], axis=1)[:, 0]
    padded = ((counts + tm - 1) // tm) * tm
    ends = jnp.cumsum(padded)
    starts = ends - padded
    slot = starts[e_flat] + rank
    ntiles = P // tm + E
    S = ntiles * tm
    tile_start = jnp.arange(ntiles, dtype=jnp.int32) * tm
    tile_expert = jnp.minimum(jnp.searchsorted(ends, tile_start, side="right"), E - 1).astype(jnp.int32)
    tile_rows = jnp.clip(counts[tile_expert] - (tile_start - starts[tile_expert]), 0, tm)
    tile_rows = jnp.where(tile_start < ends[E - 1], tile_rows, 0).astype(jnp.int32)
    last = jnp.maximum(jnp.sum((tile_rows > 0).astype(jnp.int32)) - 1, 0)
    tile_expert = jnp.where(tile_rows > 0, tile_expert, tile_expert[last])
    pair = jnp.zeros((S,), jnp.int32).at[slot].set(jnp.arange(P, dtype=jnp.int32))
    token, choice = pair // 2, pair % 2
    src = token
    dst = choice * T + token
    gate = wts.reshape(P)[pair]
    return tile_expert, tile_rows, src.reshape(ntiles, 1, tm), dst.reshape(ntiles, 1, tm), gate.reshape(S, 1)


def _experts_kernel(te_ref, tr_ref, src_ref, srcn_ref, dst_ref, gate_ref, h_hbm, w1_ref, w3_ref, w2_ref, y_hbm,
                    rows_ref, hb_ref, acc_ref, sem_in, sem_out, *, nf, tm, ntiles, gsteps):
    t = pl.program_id(0)
    f = pl.program_id(1)
    nrows = tr_ref[t]
    valid = nrows > 0
    slot = t % 2
    chunk = tm // gsteps

    def gather_row(idx_ref, r, s):
        return pltpu.make_async_copy(h_hbm.at[pl.ds(idx_ref[0, 0, r], 1)], rows_ref.at[s, pl.ds(r, 1)], sem_in)

    def scatter_wait(n, s):
        nfull = pl.multiple_of((n // SUBLANE) * SUBLANE, SUBLANE)

        @pl.when(nfull > 0)
        def _():
            pltpu.make_async_copy(rows_ref.at[s, pl.ds(0, nfull)], y_hbm.at[pl.ds(0, nfull)], sem_out).wait()

        def single(r, carry):
            pltpu.make_async_copy(rows_ref.at[s, pl.ds(r, 1)], y_hbm.at[pl.ds(0, 1)], sem_out).wait()
            return carry

        lax.fori_loop(nfull, n, single, 0)

    @pl.when((t == 0) & (f == 0))
    def _():
        def group(i, carry):
            for u in range(ROW_UNROLL):
                gather_row(src_ref, i * ROW_UNROLL + u, 0).start()
            return carry

        lax.fori_loop(0, tm // ROW_UNROLL, group, 0)

    @pl.when(valid & (f == 0))
    def _():
        pltpu.make_async_copy(h_hbm.at[pl.ds(0, tm)], rows_ref.at[slot], sem_in).wait()
        hb_ref[...] = rows_ref[slot].astype(BF16)
        acc_ref[...] = jnp.zeros(acc_ref.shape, F32)

    @pl.when((f == 1) & (t >= 1))
    def _():
        scatter_wait(tr_ref[jnp.maximum(t - 1, 0)], 1 - slot)

    def matmuls():
        h = hb_ref[...]
        a = _silu(jnp.dot(h, w1_ref[0], preferred_element_type=F32)) * jnp.dot(h, w3_ref[0], preferred_element_type=F32)
        acc_ref[...] += jnp.dot(a.astype(BF16), w2_ref[0], preferred_element_type=F32)

    def prefetch_chunk():
        base = (f - 1) * chunk
        for u in range(chunk):
            gather_row(srcn_ref, base + u, 1 - slot).start()

    prefetch = (tr_ref[jnp.minimum(t + 1, ntiles - 1)] > 0) & (t + 1 < ntiles) & (f >= 1) & (f <= gsteps)

    @pl.when(valid & prefetch)
    def _():
        matmuls()
        prefetch_chunk()

    @pl.when(valid & jnp.logical_not(prefetch))
    def _():
        matmuls()

    @pl.when(jnp.logical_not(valid) & prefetch)
    def _():
        prefetch_chunk()

    @pl.when(valid & (f == nf - 1))
    def _():
        rows_ref[slot] = acc_ref[...] * gate_ref[...]

        def scatter_row(r):
            return pltpu.make_async_copy(rows_ref.at[slot, pl.ds(r, 1)],
                                         y_hbm.at[pl.ds(dst_ref[0, 0, r], 1)], sem_out)

        ngroups = nrows // ROW_UNROLL

        def group(i, carry):
            for u in range(ROW_UNROLL):
                scatter_row(i * ROW_UNROLL + u).start(priority=u % 2)
            return carry

        def single(r, carry):
            scatter_row(r).start()
            return carry

        lax.fori_loop(0, ngroups, group, 0)
        lax.fori_loop(ngroups * ROW_UNROLL, nrows, single, 0)

    @pl.when(valid & (f == nf - 1) & (t == ntiles - 1))
    def _():
        scatter_wait(nrows, slot)


def _experts(h, plan, w13, w2, e_off, tm):
    T, D = h.shape
    Fd = w2.shape[1]
    tile_expert, tile_rows, src, dst, gate = plan
    tile_expert = tile_expert + e_off
    ntiles = src.shape[0]
    tf = _tile(Fd, TF, LANE)
    nf = Fd // tf
    assert nf >= 2, "the gather of the next tile is spread over the steps after the first"
    gsteps = 1
    while gsteps * 2 <= nf - 1 and tm % (gsteps * 2) == 0:
        gsteps *= 2
    fidx = lambda t, f, tv: jnp.where(tv[t] > 0, f, nf - 1)
    grid_spec = pltpu.PrefetchScalarGridSpec(
        num_scalar_prefetch=2,
        grid=(ntiles, nf),
        in_specs=[pl.BlockSpec((1, 1, tm), lambda t, f, te, tv: (t, 0, 0), memory_space=pltpu.SMEM),
                  pl.BlockSpec((1, 1, tm), lambda t, f, te, tv: (jnp.minimum(t + 1, ntiles - 1), 0, 0),
                               memory_space=pltpu.SMEM),
                  pl.BlockSpec((1, 1, tm), lambda t, f, te, tv: (t, 0, 0), memory_space=pltpu.SMEM),
                  pl.BlockSpec((tm, 1), lambda t, f, te, tv: (t, 0)),
                  pl.BlockSpec(memory_space=pl.ANY),
                  pl.BlockSpec((1, D, tf), lambda t, f, te, tv: (te[t], 0, fidx(t, f, tv))),
                  pl.BlockSpec((1, D, tf), lambda t, f, te, tv: (te[t], 0, nf + fidx(t, f, tv))),
                  pl.BlockSpec((1, tf, D), lambda t, f, te, tv: (te[t], fidx(t, f, tv), 0))],
        out_specs=pl.BlockSpec(memory_space=pl.ANY),
        scratch_shapes=[pltpu.VMEM((2, tm, D), F32), pltpu.VMEM((tm, D), BF16), pltpu.VMEM((tm, D), F32),
                        pltpu.SemaphoreType.DMA(()), pltpu.SemaphoreType.DMA(())],
    )
    return pl.pallas_call(
        functools.partial(_experts_kernel, nf=nf, tm=tm, ntiles=ntiles, gsteps=gsteps),
        grid_spec=grid_spec,
        out_shape=jax.ShapeDtypeStruct((2 * T, D), F32),
        compiler_params=pltpu.CompilerParams(dimension_semantics=("arbitrary", "arbitrary"),
                                             vmem_limit_bytes=VMEM_LIMIT),
        name="experts",
    )(tile_expert, tile_rows, src, src, dst, gate, h, w13, w13, w2)


def _combine_kernel(x_ref, y0_ref, y1_ref, gate_ref, g_ref, b_ref, o_ref, *, alpha):
    o_ref[0] = _res_ln(x_ref[0], y0_ref[...] + y1_ref[...], gate_ref[0], g_ref[...], b_ref[...], alpha)


def _combine_ln(x, y, gate, g, b, *, ctx, alpha):
    B, R, D = x.shape
    tm = _tile(R, TM)
    nb = R // tm
    row = _mod_row(ctx, B)
    return pl.pallas_call(
        functools.partial(_combine_kernel, alpha=alpha),
        grid=(B, nb),
        in_specs=[pl.BlockSpec((1, tm, D), lambda b, i: (b, i, 0)),
                  pl.BlockSpec((tm, D), lambda b, i: (b * nb + i, 0)),
                  pl.BlockSpec((tm, D), lambda b, i: (B * nb + b * nb + i, 0)),
                  pl.BlockSpec((1, 1, D), lambda b, i: (row(b), 0, 0)),
                  pl.BlockSpec((1, D), lambda b, i: (0, 0)),
                  pl.BlockSpec((1, D), lambda b, i: (0, 0))],
        out_specs=pl.BlockSpec((1, tm, D), lambda b, i: (b, i, 0)),
        out_shape=jax.ShapeDtypeStruct((B, R, D), F32),
        compiler_params=_params("parallel", "parallel"),
        name="combine_ln",
    )(x, y, y, gate, g.reshape(1, D), b.reshape(1, D))


def _moe(x, router, w13, w2, slot, mod, g, b, *, ctx, alpha):
    B, R, D = x.shape
    E = router.shape[1]
    T = B * R
    tm = _tile(2 * T, TME)
    h, idx, wts = _router(x, router, mod, ctx=ctx)
    plan = _route_plan(idx.reshape(T, 2), wts.reshape(T, 2), E, tm)
    y = _experts(h.reshape(T, D), plan, w13, w2, slot * E, tm)
    return _combine_ln(x, y, mod[2], g, b, ctx=ctx, alpha=alpha)


def _mix_diff(xl, xc, mod, need_ctx, layer, wqkv, lam, subln_g, rope128):
    D = xl.shape[2]
    H = D // (2 * HEAD)
    lam_init = 0.8 - 0.6 * math.exp(-0.3 * layer)
    w = wqkv.astype(BF16)
    qs = HEAD ** -0.5 * LOG2E
    qkv_l = _proj(xl, w, mod=mod, ctx=False, rope=rope128, rope_cols=(0, 2 * D), q_cols=D, q_scale=qs)
    qkv_c = _proj(xc, w, mod=mod, ctx=True, q_cols=D, q_scale=qs)
    o_l = _diff_attn(qkv_l, qkv_c, lam, subln_g, lam_init, H)
    o_c = _diff_attn(qkv_c, None, lam, subln_g, lam_init, H) if need_ctx else None
    return o_l, o_c


def _mix_window(xl, xc, mod, need_ctx, wqkv, sink, rope128):
    D = xl.shape[2]
    H = D // HEAD
    G = H // 4
    w = wqkv.astype(BF16)
    qkv_l = _proj(xl, w, mod=mod, ctx=False, rope=rope128, rope_cols=(0, (H + G) * HEAD))
    qkv_c = _proj(xc, w, mod=mod, ctx=True)
    o_l = _win_attn(qkv_l, qkv_c, sink, H, G)
    o_c = _sink_attn(qkv_c, sink, H, G) if need_ctx else None
    return o_l, o_c


def _mix_mla(xl, xc, mod, need_ctx, w_down, gq, gkv, w_uq, w_ukv, rope64):
    D = xl.shape[2]
    H = D // HEAD
    rq, rkv = gq.shape[0], gkv.shape[0]
    wd = jnp.pad(w_down, ((0, 0), (0, LANE - MLA_ROPE))).astype(BF16)
    wq3 = w_uq.reshape(rq, H, HEAD + MLA_ROPE)
    wq = jnp.concatenate(
        [wq3[:, :, :HEAD].reshape(rq, H * HEAD),
         jnp.pad(wq3[:, :, HEAD:], ((0, 0), (0, 0), (0, HEAD - MLA_ROPE))).reshape(rq, H * HEAD)], axis=1).astype(BF16)
    wkv3 = w_ukv.reshape(rkv, H, 2 * HEAD)
    wkv = jnp.concatenate([wkv3[:, :, :HEAD].reshape(rkv, H * HEAD),
                           wkv3[:, :, HEAD:].reshape(rkv, H * HEAD)], axis=1).astype(BF16)
    cq_l, ckv_l, kr_l = _mla_down(xl, wd, gq, gkv, mod, ctx=False, rope=rope64)
    cq_c, ckv_c, kr_c = _mla_down(xc, wd, gq, gkv, mod, ctx=True, rope=None)
    qs = (HEAD + MLA_ROPE) ** -0.5 * LOG2E
    q_l = _proj(cq_l, wq, rope=rope64, rope_cols=(H * HEAD, 2 * H * HEAD), q_cols=2 * H * HEAD, q_scale=qs)
    kv_l = _proj(ckv_l, wkv)
    kv_c = _proj(ckv_c, wkv)
    o_l = _mla_attn(q_l, kv_l, kr_l, kv_c, kr_c, H)
    o_c = None
    if need_ctx:
        q_c = _proj(cq_c, wq, q_cols=2 * H * HEAD, q_scale=qs)
        o_c = _mla_attn(q_c, kv_c, kr_c, None, None, H)
    return o_l, o_c


def kernel(x, c, ctx, c_ctx, mod_w, mod_b, ln1_g, ln1_b, ln2_g, ln2_b, da_wqkv, da_lambda, da_subln_g, da_wo, wa_wqkv, wa_sink, wa_wo, mla_w_down, mla_q_norm_g, mla_kv_norm_g, mla_w_uq, mla_w_ukv, mla_wo, ffn_w13, ffn_w2, moe_router, moe_w13, moe_w2):
    B, n, D = x.shape
    depth = mod_w.shape[0]
    alpha = (2.0 * depth) ** 0.25
    rows = SUBLANE * (-(-(B + 1) // SUBLANE))
    cc = jnp.concatenate([c, c_ctx[None, :], jnp.zeros((rows - B - 1, D), F32)], axis=0)
    mod_all = _modulation(cc, mod_w, mod_b)
    rope128 = _rope_tables(n, HEAD)
    rope64 = _rope_tables(n, MLA_ROPE)
    ffn_w13_b, ffn_w2_b = ffn_w13.astype(BF16), ffn_w2.astype(BF16)
    moe_w13_b = moe_w13.astype(BF16).reshape((-1,) + moe_w13.shape[2:])
    moe_w2_b = moe_w2.astype(BF16).reshape((-1,) + moe_w2.shape[2:])
    xl, xc = x, ctx
    for i in range(depth):
        need_ctx = i < depth - 1
        mod = [mod_all[i, :, k * D:(k + 1) * D].reshape(rows, 1, D) for k in range(6)]
        kind, slot = i % N_MIXERS, i // N_MIXERS
        if kind == 0:
            o_l, o_c = _mix_diff(xl, xc, mod[0:2], need_ctx, i, da_wqkv[slot], da_lambda[slot],
                                 da_subln_g[slot], rope128)
            wo = da_wo[slot]
        elif kind == 1:
            o_l, o_c = _mix_window(xl, xc, mod[0:2], need_ctx, wa_wqkv[slot], wa_sink[slot], rope128)
            wo = wa_wo[slot]
        else:
            o_l, o_c = _mix_mla(xl, xc, mod[0:2], need_ctx, mla_w_down[slot], mla_q_norm_g[slot],
                                mla_kv_norm_g[slot], mla_w_uq[slot], mla_w_ukv[slot], rope64)
            wo = mla_wo[slot]
        wo = wo.astype(BF16)
        xl = _oproj_ln(xl, o_l, wo, mod[2], ln1_g[i], ln1_b[i], ctx=False, alpha=alpha)
        if need_ctx:
            xc = _oproj_ln(xc, o_c, wo, mod[2], ln1_g[i], ln1_b[i], ctx=True, alpha=alpha)
        cslot = i // 2
        if i % 2 == 0:
            xl = _ffn(xl, ffn_w13_b, ffn_w2_b, cslot, mod[3:6], ln2_g[i], ln2_b[i], ctx=False, alpha=alpha)
            if need_ctx:
                xc = _ffn(xc, ffn_w13_b, ffn_w2_b, cslot, mod[3:6], ln2_g[i], ln2_b[i], ctx=True, alpha=alpha)
        else:
            xl = _moe(xl, moe_router[cslot], moe_w13_b, moe_w2_b, cslot, mod[3:6], ln2_g[i], ln2_b[i],
                      ctx=False, alpha=alpha)
            if need_ctx:
                xc = _moe(xc, moe_router[cslot], moe_w13_b, moe_w2_b, cslot, mod[3:6], ln2_g[i], ln2_b[i],
                          ctx=True, alpha=alpha)
    return xl
```
